```python
import jax, jax.numpy as jnp
from jax import lax
import numpy as np

D_MODEL = 1024
BATCH = 8
SEQ = 2048
DEPTH = 2
DEC_BATCH = 128
DEC_SEQ = 1
PAST_LEN = 2048
PAGE_SIZE = 128

N_HEADS = 8
HEAD_DIM = D_MODEL // N_HEADS
ROT_DIM = HEAD_DIM // 4
ROPE_THETA = 500000.0
MOBA_BLOCK = 256
MOBA_TOP_K = 3
Q_CHUNK = 16
CONV_W = 3
N_CONV_LAYERS = (DEPTH + 1) // 2
N_ATTN_LAYERS = DEPTH // 2
DN_ALPHA = (2.0 * DEPTH) ** 0.25
DN_BETA = (8.0 * DEPTH) ** -0.25
LN_EPS = 1e-5

kernel_name = 'hybrid_shortconv_moba_deepnorm_step'


def _layernorm(x, g, b):
    xf = x.astype(jnp.float32)
    mu = xf.mean(-1, keepdims=True)
    var = jnp.square(xf - mu).mean(-1, keepdims=True)
    return ((xf - mu) * lax.rsqrt(var + LN_EPS) * g + b).astype(x.dtype)


def _rope(x, pos):
    half = ROT_DIM // 2
    inv = ROPE_THETA ** (-jnp.arange(half, dtype=jnp.float32) * 2.0 / ROT_DIM)
    ang = pos.astype(jnp.float32)[:, None] * inv[None, :]
    cos = jnp.cos(ang)[None, :, None, :]
    sin = jnp.sin(ang)[None, :, None, :]
    xr = x[..., :ROT_DIM].astype(jnp.float32)
    x1, x2 = xr[..., :half], xr[..., half:]
    rot = jnp.concatenate([x1 * cos - x2 * sin, x2 * cos + x1 * sin], axis=-1).astype(x.dtype)
    return jnp.concatenate([rot, x[..., ROT_DIM:]], axis=-1)


def _short_conv_mixer(x, prev, w_in, w_conv, w_out):
    T = x.shape[1]
    b_gate, c_gate, h, z = jnp.split(x @ w_in, 4, axis=-1)
    u = c_gate * h
    up = jnp.concatenate([prev.astype(u.dtype), u], axis=1)
    conv = sum(w_conv[j] * up[:, j:j + T] for j in range(CONV_W))
    y = (b_gate * conv * jax.nn.silu(z)) @ w_out
    return y, up[:, -(CONV_W - 1):]


def _attn_inputs(x, pos, w_in):
    B, T, _ = x.shape
    q, k, v, z = jnp.split(x @ w_in, 4, axis=-1)
    heads = lambda t: t.reshape(B, T, N_HEADS, HEAD_DIM)
    return _rope(heads(q), pos), _rope(heads(k), pos), heads(v), z


def _moba_core(q, q_pos, kmean, gather_sel, k_own, v_own, own_pos):
    B, Q, H, _ = q.shape
    scale = HEAD_DIM ** -0.5
    s_own = jnp.einsum('bqhd,bkhd->bhqk', q, k_own, preferred_element_type=jnp.float32) * scale
    own_ok = own_pos[None, :] <= q_pos[:, None]
    s_own = jnp.where(own_ok[None, None], s_own, -jnp.inf)
    n_cand = kmean.shape[1]
    k_sel = min(MOBA_TOP_K, n_cand)
    if k_sel == 0:
        p = jax.nn.softmax(s_own, axis=-1).astype(v_own.dtype)
        return jnp.einsum('bhqk,bkhd->bqhd', p, v_own)
    qb = (q_pos // MOBA_BLOCK)[None, None, :, None]
    gate = jnp.einsum('bqhd,bnhd->bhqn', q, kmean, preferred_element_type=jnp.float32)
    gate = jnp.where(jnp.arange(n_cand)[None, None, None, :] < qb, gate, -jnp.inf)
    _, idx = lax.top_k(gate, k_sel)
    sel_ok = jnp.repeat(idx < qb, MOBA_BLOCK, axis=-1)
    pos = (idx[..., None] * MOBA_BLOCK + jnp.arange(MOBA_BLOCK)).reshape(B, H, Q, k_sel * MOBA_BLOCK)
    ks, vs = gather_sel(pos)
    s_sel = jnp.einsum('bqhd,bhqld->bhql', q, ks, preferred_element_type=jnp.float32) * scale
    s_sel = jnp.where(sel_ok, s_sel, -jnp.inf)
    p = jax.nn.softmax(jnp.concatenate([s_sel, s_own], axis=-1), axis=-1).astype(vs.dtype)
    n_sel = pos.shape[-1]
    return (jnp.einsum('bhql,bhqld->bqhd', p[..., :n_sel], vs)
            + jnp.einsum('bhqk,bkhd->bqhd', p[..., n_sel:], v_own))


def _moba_prompt(q, k, v):
    B, T, H, hd = q.shape
    nb = -(-T // MOBA_BLOCK)
    padn = nb * MOBA_BLOCK - T
    kpad = jnp.pad(k, ((0, 0), (0, padn), (0, 0), (0, 0)))
    vpad = jnp.pad(v, ((0, 0), (0, padn), (0, 0), (0, 0)))
    kmean = kpad.astype(jnp.float32).reshape(B, nb, MOBA_BLOCK, H, hd).mean(2).astype(q.dtype)
    bidx = jnp.arange(B)[:, None, None, None]
    hidx = jnp.arange(H)[None, :, None, None]

    def gather_sel(pos):
        return kpad[bidx, pos, hidx], vpad[bidx, pos, hidx]

    def chunk(args):
        qc, pc = args
        start = (pc[0] // MOBA_BLOCK) * MOBA_BLOCK
        k_own = lax.dynamic_slice_in_dim(kpad, start, MOBA_BLOCK, axis=1)
        v_own = lax.dynamic_slice_in_dim(vpad, start, MOBA_BLOCK, axis=1)
        own_pos = start + jnp.arange(MOBA_BLOCK, dtype=jnp.int32)
        return _moba_core(qc, pc, kmean, gather_sel, k_own, v_own, own_pos)

    n_ch = T // Q_CHUNK
    qs = q.reshape(B, n_ch, Q_CHUNK, H, hd).transpose(1, 0, 2, 3, 4)
    ps = jnp.arange(T, dtype=jnp.int32).reshape(n_ch, Q_CHUNK)
    out = lax.map(chunk, (qs, ps))
    return out.transpose(1, 0, 2, 3, 4).reshape(B, T, H, hd)


def _moba_sample(q, k_new, v_new, cache_k, cache_v, layer, page_table):
    B, Q, H, hd = q.shape
    ppb = MOBA_BLOCK // PAGE_SIZE
    n_full = PAST_LEN // MOBA_BLOCK
    own_start = n_full * MOBA_BLOCK
    if n_full > 0:
        rows = cache_k[layer, page_table[:, :n_full * ppb]]
        kmean = rows.astype(jnp.float32).reshape(B, n_full, MOBA_BLOCK, H, hd).mean(2).astype(q.dtype)
    else:
        kmean = jnp.zeros((B, 0, H, hd), q.dtype)
    own_pages = page_table[:, own_start // PAGE_SIZE:]
    n_own_past = PAST_LEN - own_start
    k_own = jnp.concatenate([cache_k[layer, own_pages].reshape(B, n_own_past, H, hd).astype(k_new.dtype), k_new], axis=1)
    v_own = jnp.concatenate([cache_v[layer, own_pages].reshape(B, n_own_past, H, hd).astype(v_new.dtype), v_new], axis=1)
    own_pos = own_start + jnp.arange(n_own_past + Q, dtype=jnp.int32)
    q_pos = PAST_LEN + jnp.arange(Q, dtype=jnp.int32)
    bidx = jnp.arange(B)[:, None, None, None]
    hidx = jnp.arange(H)[None, :, None, None]

    def gather_sel(pos):
        phys = page_table[bidx, pos // PAGE_SIZE]
        off = pos % PAGE_SIZE
        return cache_k[layer, phys, off, hidx], cache_v[layer, phys, off, hidx]

    return _moba_core(q, q_pos, kmean, gather_sel, k_own, v_own, own_pos)


def setup_inputs(seed: int = 0) -> dict:
    key = jax.random.key(seed)
    ks = jax.random.split(key, 13)
    n_pages = PAST_LEN // PAGE_SIZE
    n_used = DEC_BATCH * n_pages
    n_phys = n_used + max(1, n_used // 4)
    d = D_MODEL
    f32 = jnp.float32
    s_in = d ** -0.5
    s_out = d ** -0.5 * DN_BETA
    x_prompt = jax.random.normal(ks[0], (BATCH, SEQ, d), f32)
    x_sample = jax.random.normal(ks[1], (DEC_BATCH, DEC_SEQ, d), f32)
    state_conv = jax.random.normal(ks[2], (N_CONV_LAYERS, DEC_BATCH, CONV_W - 1, d), f32)
    cache_k = jax.random.normal(ks[3], (N_ATTN_LAYERS, n_phys, PAGE_SIZE, N_HEADS, HEAD_DIM), f32)
    cache_v = jax.random.normal(ks[4], (N_ATTN_LAYERS, n_phys, PAGE_SIZE, N_HEADS, HEAD_DIM), f32)
    page_table = jax.random.permutation(ks[5], n_phys)[:n_used].reshape(DEC_BATCH, n_pages).astype(jnp.int32)
    w_in_conv = jax.random.normal(ks[6], (N_CONV_LAYERS, d, 4 * d), f32) * s_in
    w_conv = jax.random.normal(ks[7], (N_CONV_LAYERS, CONV_W, d), f32) * CONV_W ** -0.5
    w_out_conv = jax.random.normal(ks[8], (N_CONV_LAYERS, d, d), f32) * s_out
    w_in_attn = jax.random.normal(ks[9], (N_ATTN_LAYERS, d, 4 * d), f32) * s_in
    w_out_attn = jax.random.normal(ks[10], (N_ATTN_LAYERS, d, d), f32) * s_out
    ln_g = 1.0 + 0.02 * jax.random.normal(ks[11], (DEPTH, d), f32)
    ln_b = 0.02 * jax.random.normal(ks[12], (DEPTH, d), f32)
    return {'x_prompt': x_prompt, 'x_sample': x_sample, 'state_conv': state_conv,
            'cache_k': cache_k, 'cache_v': cache_v, 'page_table': page_table,
            'w_in_conv': w_in_conv, 'w_conv': w_conv, 'w_out_conv': w_out_conv,
            'w_in_attn': w_in_attn, 'w_out_attn': w_out_attn, 'ln_g': ln_g, 'ln_b': ln_b}


def reference(x_prompt, x_sample, state_conv, cache_k, cache_v, page_table,
              w_in_conv, w_conv, w_out_conv, w_in_attn, w_out_attn, ln_g, ln_b):
    pos_p = jnp.arange(x_prompt.shape[1], dtype=jnp.int32)
    pos_s = PAST_LEN + jnp.arange(x_sample.shape[1], dtype=jnp.int32)
    xp, xs = x_prompt, x_sample
    conv_p, conv_s, k_p, v_p, k_s, v_s = [], [], [], [], [], []
    for i in range(DEPTH):
        j = i // 2
        if i % 2 == 0:
            zeros = jnp.zeros((xp.shape[0], CONV_W - 1, D_MODEL), xp.dtype)
            yp, sp = _short_conv_mixer(xp, zeros, w_in_conv[j], w_conv[j], w_out_conv[j])
            ys, ss = _short_conv_mixer(xs, state_conv[j], w_in_conv[j], w_conv[j], w_out_conv[j])
            conv_p.append(sp)
            conv_s.append(ss)
        else:
            Bp, Tp, _ = xp.shape
            q, k, v, z = _attn_inputs(xp, pos_p, w_in_attn[j])
            o = _moba_prompt(q, k, v)
            yp = (o.reshape(Bp, Tp, D_MODEL) * jax.nn.silu(z)) @ w_out_attn[j]
            k_p.append(k.reshape(Bp, Tp // PAGE_SIZE, PAGE_SIZE, N_HEADS, HEAD_DIM))
            v_p.append(v.reshape(Bp, Tp // PAGE_SIZE, PAGE_SIZE, N_HEADS, HEAD_DIM))
            Bs, Ts, _ = xs.shape
            qs_, ks_, vs_, zs = _attn_inputs(xs, pos_s, w_in_attn[j])
            os_ = _moba_sample(qs_, ks_, vs_, cache_k, cache_v, j, page_table)
            ys = (os_.reshape(Bs, Ts, D_MODEL) * jax.nn.silu(zs)) @ w_out_attn[j]
            k_s.append(ks_)
            v_s.append(vs_)
        xp = _layernorm(DN_ALPHA * xp + yp, ln_g[i], ln_b[i])
        xs = _layernorm(DN_ALPHA * xs + ys, ln_g[i], ln_b[i])
    return (xp, xs, jnp.stack(conv_p), jnp.stack(conv_s), jnp.stack(k_p), jnp.stack(v_p), jnp.stack(k_s), jnp.stack(v_s))
```

```python
import functools

import numpy as np
import jax
import jax.numpy as jnp
from jax import lax
from jax.experimental import pallas as pl
from jax.experimental.pallas import tpu as pltpu

D_MODEL = 1024
DEPTH = 2
PAST_LEN = 2048
PAGE_SIZE = 128
N_HEADS = 8
HEAD_DIM = D_MODEL // N_HEADS
ROT_DIM = HEAD_DIM // 4
ROT_HALF = ROT_DIM // 2
ROPE_THETA = 500000.0
MOBA_BLOCK = 256
MOBA_TOP_K = 3
CONV_W = 3
DN_ALPHA = (2.0 * DEPTH) ** 0.25
LN_EPS = 1e-5
ATTN_SCALE = HEAD_DIM ** -0.5
PAGES_PER_BLOCK = MOBA_BLOCK // PAGE_SIZE

SUBLANES = 8
ROW_TILE = 512
COL_CHUNK = 512
VMEM_LIMIT = 56 * 1024 * 1024

BF16 = jnp.bfloat16
F32 = jnp.float32


def _dot(a, b):
    return jnp.dot(a, b, preferred_element_type=F32)


def _dot_nt(a, b):
    return lax.dot_general(a, b, (((1,), (1,)), ((), ())), preferred_element_type=F32)


def _layernorm(r, g, b):
    mu = jnp.mean(r, axis=-1, keepdims=True)
    c = r - mu
    var = jnp.mean(c * c, axis=-1, keepdims=True)
    return c * lax.rsqrt(var + LN_EPS) * g + b


def _silu(z):
    return z * jax.nn.sigmoid(z)


def _const_spec(shape, grid_rank):
    return pl.BlockSpec(shape, lambda *_: (0,) * len(shape), pipeline_mode=pl.Buffered(1))


def _conv_prompt_kernel(x_ref, win_ref, wconv_ref, wout_ref, g_ref, b_ref,
                        y_ref, state_ref, ext_ref, gate_ref, *, tm):
    t = pl.program_id(1)

    @pl.when(t == 0)
    def _():
        ext_ref[0:SUBLANES, :] = jnp.zeros((SUBLANES, D_MODEL), F32)

    x = x_ref[0]
    xb = x.astype(BF16)
    for c in range(D_MODEL // COL_CHUNK):
        lo = c * COL_CHUNK
        cols = slice(lo, lo + COL_CHUNK)
        b_gate = _dot(xb, win_ref[:, 0 * D_MODEL + lo:0 * D_MODEL + lo + COL_CHUNK])
        c_gate = _dot(xb, win_ref[:, 1 * D_MODEL + lo:1 * D_MODEL + lo + COL_CHUNK])
        h = _dot(xb, win_ref[:, 2 * D_MODEL + lo:2 * D_MODEL + lo + COL_CHUNK])
        z = _dot(xb, win_ref[:, 3 * D_MODEL + lo:3 * D_MODEL + lo + COL_CHUNK])
        u = c_gate * h
        ext_ref[SUBLANES:SUBLANES + tm, cols] = u
        ext = ext_ref[:, cols]
        u1 = pltpu.roll(ext, 1, 0)[SUBLANES:]
        u2 = pltpu.roll(ext, 2, 0)[SUBLANES:]
        conv = (wconv_ref[0:1, cols] * u2 + wconv_ref[1:2, cols] * u1
                + wconv_ref[2:3, cols] * u)
        gate_ref[:, cols] = (b_gate * conv * _silu(z)).astype(BF16)
    tail = ext_ref[tm:tm + SUBLANES, :]
    ext_ref[0:SUBLANES, :] = tail
    state_ref[0, 0] = tail
    y = _dot(gate_ref[...], wout_ref[...])
    y_ref[0] = _layernorm(DN_ALPHA * x + y, g_ref[...], b_ref[...])


def _conv_prompt(x, w_in, w_conv, w_out, g, b):
    bsz, seq, _ = x.shape
    tm = ROW_TILE
    assert seq % tm == 0
    return pl.pallas_call(
        functools.partial(_conv_prompt_kernel, tm=tm),
        grid=(bsz, seq // tm),
        in_specs=[
            pl.BlockSpec((1, tm, D_MODEL), lambda i, j: (i, j, 0)),
            _const_spec((D_MODEL, 4 * D_MODEL), 2),
            _const_spec((CONV_W, D_MODEL), 2),
            _const_spec((D_MODEL, D_MODEL), 2),
            _const_spec((1, D_MODEL), 2),
            _const_spec((1, D_MODEL), 2),
        ],
        out_specs=[
            pl.BlockSpec((1, tm, D_MODEL), lambda i, j: (i, j, 0)),
            pl.BlockSpec((1, 1, SUBLANES, D_MODEL), lambda i, j: (0, i, 0, 0)),
        ],
        out_shape=[
            jax.ShapeDtypeStruct((bsz, seq, D_MODEL), F32),
            jax.ShapeDtypeStruct((1, bsz, SUBLANES, D_MODEL), F32),
        ],
        scratch_shapes=[
            pltpu.VMEM((tm + SUBLANES, D_MODEL), F32),
            pltpu.VMEM((tm, D_MODEL), BF16),
        ],
        compiler_params=pltpu.CompilerParams(
            dimension_semantics=("arbitrary", "arbitrary"),
            vmem_limit_bytes=VMEM_LIMIT),
        name="conv_prompt",
    )(x, w_in, w_conv, w_out, g, b)


def _conv_sample_kernel(x_ref, prev_ref, win_ref, wconv_ref, wout_ref, g_ref, b_ref,
                        y_ref, state_ref):
    x = x_ref[...]
    xb = x.astype(BF16)
    b_gate = _dot(xb, win_ref[:, 0 * D_MODEL:1 * D_MODEL])
    c_gate = _dot(xb, win_ref[:, 1 * D_MODEL:2 * D_MODEL])
    h = _dot(xb, win_ref[:, 2 * D_MODEL:3 * D_MODEL])
    z = _dot(xb, win_ref[:, 3 * D_MODEL:4 * D_MODEL])
    u = c_gate * h
    prev0 = prev_ref[:, 0:D_MODEL]
    prev1 = prev_ref[:, D_MODEL:2 * D_MODEL]
    conv = wconv_ref[0:1, :] * prev0 + wconv_ref[1:2, :] * prev1 + wconv_ref[2:3, :] * u
    gate = (b_gate * conv * _silu(z)).astype(BF16)
    y = _dot(gate, wout_ref[...])
    y_ref[...] = _layernorm(DN_ALPHA * x + y, g_ref[...], b_ref[...])
    state_ref[:, 0:D_MODEL] = prev1
    state_ref[:, D_MODEL:2 * D_MODEL] = u


def _conv_sample(x, prev, w_in, w_conv, w_out, g, b):
    rows = x.shape[0]
    return pl.pallas_call(
        _conv_sample_kernel,
        out_shape=[
            jax.ShapeDtypeStruct((rows, D_MODEL), F32),
            jax.ShapeDtypeStruct((rows, 2 * D_MODEL), F32),
        ],
        compiler_params=pltpu.CompilerParams(vmem_limit_bytes=VMEM_LIMIT),
        name="conv_sample",
    )(x, prev, w_in, w_conv, w_out, g, b)


def _rope(x, cos_ref, sin_up_ref, sin_dn_ref):
    cos, sin_up, sin_dn = cos_ref[...], sin_up_ref[...], sin_dn_ref[...]
    heads = []
    for hd in range(N_HEADS):
        xh = x[:, hd * HEAD_DIM:(hd + 1) * HEAD_DIM]
        heads.append(xh * cos
                     + pltpu.roll(xh, ROT_HALF, 1) * sin_up
                     + pltpu.roll(xh, HEAD_DIM - ROT_HALF, 1) * sin_dn)
    return jnp.concatenate(heads, axis=1)


def _store_paged(ref, val):
    for pg in range(ref.shape[0]):
        for hd in range(N_HEADS):
            ref[pg, :, hd, :] = val[pg * PAGE_SIZE:(pg + 1) * PAGE_SIZE,
                                    hd * HEAD_DIM:(hd + 1) * HEAD_DIM].astype(ref.dtype)


def _attn_proj_prompt_kernel(x_ref, win_ref, cos_ref, sin_up_ref, sin_dn_ref,
                             q_ref, k_ref, v_ref, kb_ref, vb_ref, sz_ref, kmean_ref, *, tm):
    xb = x_ref[0].astype(BF16)
    q = _rope(_dot(xb, win_ref[:, 0 * D_MODEL:1 * D_MODEL]), cos_ref, sin_up_ref, sin_dn_ref)
    q_ref[0] = (q * ATTN_SCALE).astype(BF16)
    k = _rope(_dot(xb, win_ref[:, 1 * D_MODEL:2 * D_MODEL]), cos_ref, sin_up_ref, sin_dn_ref)
    _store_paged(k_ref.at[0], k)
    kb_ref[0] = k.astype(BF16)
    t = pl.program_id(1)
    for j in range(tm // MOBA_BLOCK):
        blk = k[j * MOBA_BLOCK:(j + 1) * MOBA_BLOCK]
        kmean_ref[0, pl.ds(t * (tm // MOBA_BLOCK) + j, 1), :] = (
            jnp.sum(blk, axis=0, keepdims=True) * (1.0 / MOBA_BLOCK))
    v = _dot(xb, win_ref[:, 2 * D_MODEL:3 * D_MODEL])
    _store_paged(v_ref.at[0], v)
    vb_ref[0] = v.astype(BF16)
    z = _dot(xb, win_ref[:, 3 * D_MODEL:4 * D_MODEL])
    sz_ref[0] = _silu(z).astype(BF16)


def _paged_spec(tm):
    return pl.BlockSpec((1, tm // PAGE_SIZE, PAGE_SIZE, N_HEADS, HEAD_DIM),
                        lambda i, j: (i, j, 0, 0, 0))


def _attn_proj_prompt(x, w_in, cos, sin_up, sin_dn):
    bsz, seq, _ = x.shape
    tm = ROW_TILE
    assert seq % tm == 0 and tm % MOBA_BLOCK == 0 and seq // MOBA_BLOCK == SUBLANES
    row_spec = pl.BlockSpec((1, tm, D_MODEL), lambda i, j: (i, j, 0))
    tab_spec = pl.BlockSpec((tm, HEAD_DIM), lambda i, j: (j, 0))
    flat = lambda dt: jax.ShapeDtypeStruct((bsz, seq, D_MODEL), dt)
    paged = jax.ShapeDtypeStruct((bsz, seq // PAGE_SIZE, PAGE_SIZE, N_HEADS, HEAD_DIM), F32)
    return pl.pallas_call(
        functools.partial(_attn_proj_prompt_kernel, tm=tm),
        grid=(bsz, seq // tm),
        in_specs=[row_spec, _const_spec((D_MODEL, 4 * D_MODEL), 2),
                  tab_spec, tab_spec, tab_spec],
        out_specs=[row_spec, _paged_spec(tm), _paged_spec(tm), row_spec, row_spec, row_spec,
                   pl.BlockSpec((1, SUBLANES, D_MODEL), lambda i, j: (i, 0, 0))],
        out_shape=[flat(BF16), paged, paged, flat(BF16), flat(BF16), flat(BF16),
                   jax.ShapeDtypeStruct((bsz, SUBLANES, D_MODEL), F32)],
        compiler_params=pltpu.CompilerParams(
            dimension_semantics=("arbitrary", "arbitrary"),
            vmem_limit_bytes=VMEM_LIMIT),
        name="attn_proj_prompt",
    )(x, w_in, cos, sin_up, sin_dn)


def _attn_proj_sample_kernel(x_ref, win_ref, cos_ref, sin_up_ref, sin_dn_ref,
                             q_ref, k_ref, v_ref, sz_ref):
    xb = x_ref[...].astype(BF16)
    q = _rope(_dot(xb, win_ref[:, 0 * D_MODEL:1 * D_MODEL]), cos_ref, sin_up_ref, sin_dn_ref)
    _store_paged(q_ref, q * ATTN_SCALE)
    k = _rope(_dot(xb, win_ref[:, 1 * D_MODEL:2 * D_MODEL]), cos_ref, sin_up_ref, sin_dn_ref)
    _store_paged(k_ref, k)
    _store_paged(v_ref, _dot(xb, win_ref[:, 2 * D_MODEL:3 * D_MODEL]))
    _store_paged(sz_ref, _silu(_dot(xb, win_ref[:, 3 * D_MODEL:4 * D_MODEL])))


def _attn_proj_sample(x, w_in, cos, sin_up, sin_dn):
    rows = x.shape[0]
    assert rows % PAGE_SIZE == 0
    paged = jax.ShapeDtypeStruct((rows // PAGE_SIZE, PAGE_SIZE, N_HEADS, HEAD_DIM), F32)
    return pl.pallas_call(
        _attn_proj_sample_kernel,
        out_shape=[paged, paged, paged, paged],
        compiler_params=pltpu.CompilerParams(vmem_limit_bytes=VMEM_LIMIT),
        name="attn_proj_sample",
    )(x, w_in, cos, sin_up, sin_dn)


def _moba_prompt_kernel(q_ref, k_ref, v_ref, kmean_ref, sz_ref, o_ref, *, n_blocks):
    blk = MOBA_BLOCK
    kmean = jnp.concatenate(
        [kmean_ref[0].astype(BF16),
         jnp.zeros((HEAD_DIM - n_blocks, HEAD_DIM), BF16)], axis=0)
    lane = lax.broadcasted_iota(jnp.int32, (blk, HEAD_DIM), 1)
    row = lax.broadcasted_iota(jnp.int32, (blk, blk), 0)
    col = lax.broadcasted_iota(jnp.int32, (blk, blk), 1)
    causal = col <= row
    for i in range(n_blocks):
        qi = q_ref[0, i * blk:(i + 1) * blk, :]
        scores = []
        if i > MOBA_TOP_K:
            gate = _dot_nt(qi, kmean)
        for n in range(i):
            s = _dot_nt(qi, k_ref[0, n * blk:(n + 1) * blk, :])
            if i > MOBA_TOP_K:
                gn = gate[:, n:n + 1]
                beats = ((gate > gn) | ((gate == gn) & (lane < n))) & (lane < i)
                rank = jnp.sum(beats.astype(F32), axis=1, keepdims=True)
                s = jnp.where(rank < MOBA_TOP_K, s, -jnp.inf)
            scores.append(s)
        s_own = _dot_nt(qi, k_ref[0, i * blk:(i + 1) * blk, :])
        scores.append(jnp.where(causal, s_own, -jnp.inf))
        m = scores[0].max(axis=1, keepdims=True)
        for s in scores[1:]:
            m = jnp.maximum(m, s.max(axis=1, keepdims=True))
        acc = jnp.zeros((blk, HEAD_DIM), F32)
        denom = jnp.zeros((blk, 1), F32)
        for n, s in enumerate(scores):
            p = jnp.exp(s - m)
            denom = denom + jnp.sum(p, axis=1, keepdims=True)
            acc = acc + _dot(p.astype(BF16), v_ref[0, n * blk:(n + 1) * blk, :])
        o = acc / denom
        o_ref[0, i * blk:(i + 1) * blk, :] = (
            o * sz_ref[0, i * blk:(i + 1) * blk, :].astype(F32)).astype(BF16)


def _moba_prompt(q, k, v, kmean, sz):
    bsz, seq, _ = q.shape
    n_blocks = seq // MOBA_BLOCK
    assert n_blocks == SUBLANES
    head_spec = pl.BlockSpec((1, seq, HEAD_DIM), lambda i, j: (i, 0, j))
    return pl.pallas_call(
        functools.partial(_moba_prompt_kernel, n_blocks=n_blocks),
        grid=(bsz, N_HEADS),
        in_specs=[head_spec, head_spec, head_spec,
                  pl.BlockSpec((1, n_blocks, HEAD_DIM), lambda i, j: (i, 0, j)),
                  head_spec],
        out_specs=head_spec,
        out_shape=jax.ShapeDtypeStruct((bsz, seq, D_MODEL), BF16),
        compiler_params=pltpu.CompilerParams(
            dimension_semantics=("arbitrary", "arbitrary"),
            vmem_limit_bytes=VMEM_LIMIT),
        name="moba_prompt",
    )(q, k, v, kmean, sz)


def _moba_sample_kernel(pt_ref, q_ref, kn_ref, vn_ref, sz_ref, k0_ref, k1_ref, v0_ref, v1_ref,
                        o_ref, m_ref, l_ref, acc_ref, gate_ref, *, n_blocks):
    n = pl.program_id(1)
    tile = (N_HEADS, HEAD_DIM)
    q = q_ref[0]
    kblk = jnp.concatenate([k0_ref[...], k1_ref[...]], axis=0)
    vblk = jnp.concatenate([v0_ref[...], v1_ref[...]], axis=0)
    s = jnp.sum(kblk * q[None], axis=-1, keepdims=True)
    m = jnp.max(s, axis=0)
    p = jnp.exp(s - m[None])
    kmean = jnp.sum(kblk, axis=0) * (1.0 / MOBA_BLOCK)
    m_ref[n] = jnp.broadcast_to(m, tile)
    l_ref[n] = jnp.broadcast_to(jnp.sum(p, axis=0), tile)
    acc_ref[n] = jnp.sum(p * vblk, axis=0)
    gate_ref[n] = jnp.broadcast_to(jnp.sum(q * kmean, axis=-1, keepdims=True), tile)

    @pl.when(n == n_blocks - 1)
    def _():
        gates = gate_ref[...]
        blk_id = lax.broadcasted_iota(jnp.int32, (n_blocks,) + tile, 0)
        sel = jnp.zeros((n_blocks,) + tile, jnp.bool_)
        for j in range(n_blocks):
            gj = gates[j:j + 1]
            beats = (gates > gj) | ((gates == gj) & (blk_id < j))
            rank = jnp.sum(beats.astype(F32), axis=0, keepdims=True)
            sel = sel | ((blk_id == j) & (rank < MOBA_TOP_K))
        s_own = jnp.broadcast_to(jnp.sum(q * kn_ref[0], axis=-1, keepdims=True), tile)
        ms = jnp.where(sel, m_ref[...], -jnp.inf)
        m_fin = jnp.maximum(jnp.max(ms, axis=0), s_own)
        w = jnp.exp(ms - m_fin[None])
        w_own = jnp.exp(s_own - m_fin)
        denom = jnp.sum(w * l_ref[...], axis=0) + w_own
        numer = jnp.sum(w * acc_ref[...], axis=0) + w_own * vn_ref[0]
        o_ref[0] = (numer / denom) * sz_ref[0]


def _moba_sample(page_table, q, k_new, v_new, sz, cache_k, cache_v):
    rows = q.shape[0]
    n_blocks = PAST_LEN // MOBA_BLOCK
    assert PAST_LEN % MOBA_BLOCK == 0 and PAGES_PER_BLOCK == 2
    assert page_table.shape == (rows, n_blocks * PAGES_PER_BLOCK)
    assert cache_k.shape[0] == 1 and cache_k.shape[2:] == (PAGE_SIZE, N_HEADS, HEAD_DIM)
    row_spec = pl.BlockSpec((1, N_HEADS, HEAD_DIM), lambda i, j, pt: (i, 0, 0))
    page = lambda half: pl.BlockSpec(
        (None, None, PAGE_SIZE, N_HEADS, HEAD_DIM),
        lambda i, j, pt: (0, pt[i, PAGES_PER_BLOCK * j + half], 0, 0, 0))
    stat = pltpu.VMEM((n_blocks, N_HEADS, HEAD_DIM), F32)
    return pl.pallas_call(
        functools.partial(_moba_sample_kernel, n_blocks=n_blocks),
        grid_spec=pltpu.PrefetchScalarGridSpec(
            num_scalar_prefetch=1,
            grid=(rows, n_blocks),
            in_specs=[row_spec, row_spec, row_spec, row_spec,
                      page(0), page(1), page(0), page(1)],
            out_specs=row_spec,
            scratch_shapes=[stat, stat, stat, stat],
        ),
        out_shape=jax.ShapeDtypeStruct((rows, N_HEADS, HEAD_DIM), F32),
        compiler_params=pltpu.CompilerParams(
            dimension_semantics=("arbitrary", "arbitrary"),
            vmem_limit_bytes=VMEM_LIMIT),
        name="moba_sample",
    )(page_table, q, k_new, v_new, sz, cache_k, cache_k, cache_v, cache_v)


def _out_proj_kernel(a_ref, x_ref, w_ref, g_ref, b_ref, y_ref):
    y = _dot(a_ref[...].astype(BF16), w_ref[...])
    y_ref[...] = _layernorm(DN_ALPHA * x_ref[...] + y, g_ref[...], b_ref[...])


def _out_proj(a, x, w, g, b, *, tm):
    rows = x.shape[0]
    assert rows % tm == 0
    row_spec = pl.BlockSpec((tm, D_MODEL), lambda i: (i, 0))
    return pl.pallas_call(
        _out_proj_kernel,
        grid=(rows // tm,),
        in_specs=[row_spec, row_spec, _const_spec((D_MODEL, D_MODEL), 1),
                  _const_spec((1, D_MODEL), 1), _const_spec((1, D_MODEL), 1)],
        out_specs=row_spec,
        out_shape=jax.ShapeDtypeStruct((rows, D_MODEL), F32),
        compiler_params=pltpu.CompilerParams(
            dimension_semantics=("arbitrary",), vmem_limit_bytes=VMEM_LIMIT),
        name="out_proj",
    )(a, x, w, g, b)


def _rope_tables(pos):
    inv = ROPE_THETA ** (-np.arange(ROT_HALF, dtype=np.float64) * 2.0 / ROT_DIM)
    ang = np.asarray(pos, np.float64)[:, None] * inv[None, :]
    cos, sin = np.cos(ang), np.sin(ang)
    n = ang.shape[0]
    rest = HEAD_DIM - ROT_DIM
    cos_t = np.concatenate([cos, cos, np.ones((n, rest))], axis=1)
    sin_up = np.concatenate([np.zeros((n, ROT_HALF)), sin, np.zeros((n, rest))], axis=1)
    sin_dn = np.concatenate([-sin, np.zeros((n, ROT_HALF + rest))], axis=1)
    return tuple(jnp.asarray(t, F32) for t in (cos_t, sin_up, sin_dn))


def kernel(x_prompt, x_sample, state_conv, cache_k, cache_v, page_table,
           w_in_conv, w_conv, w_out_conv, w_in_attn, w_out_attn, ln_g, ln_b):
    bsz, seq, _ = x_prompt.shape
    rows, dec_seq, _ = x_sample.shape
    assert dec_seq == 1 and w_in_conv.shape[0] == 1 and w_in_attn.shape[0] == 1
    win_c, wout_c = w_in_conv[0].astype(BF16), w_out_conv[0].astype(BF16)
    win_a, wout_a = w_in_attn[0].astype(BF16), w_out_attn[0].astype(BF16)
    g0, b0, g1, b1 = ln_g[0:1], ln_b[0:1], ln_g[1:2], ln_b[1:2]

    xp1, conv_tail = _conv_prompt(x_prompt, win_c, w_conv[0], wout_c, g0, b0)
    conv_p = conv_tail[:, :, SUBLANES - (CONV_W - 1):, :]
    xs1, conv_s = _conv_sample(
        x_sample.reshape(rows, D_MODEL), state_conv[0].reshape(rows, (CONV_W - 1) * D_MODEL),
        win_c, w_conv[0], wout_c, g0, b0)
    conv_s = conv_s.reshape(1, rows, CONV_W - 1, D_MODEL)

    q, k_p, v_p, kb, vb, sz, kmean = _attn_proj_prompt(xp1, win_a, *_rope_tables(np.arange(seq)))
    o = _moba_prompt(q, kb, vb, kmean, sz)
    y_p = _out_proj(o.reshape(bsz * seq, D_MODEL), xp1.reshape(bsz * seq, D_MODEL),
                    wout_a, g1, b1, tm=ROW_TILE)
    y_p = y_p.reshape(bsz, seq, D_MODEL)

    qs, ks, vs, szs = _attn_proj_sample(xs1, win_a, *_rope_tables(np.full((rows,), PAST_LEN)))
    per_row = lambda t: t.reshape(rows, N_HEADS, HEAD_DIM)
    os_ = _moba_sample(page_table, per_row(qs), per_row(ks), per_row(vs), per_row(szs),
                       cache_k, cache_v)
    y_s = _out_proj(os_.reshape(rows, D_MODEL), xs1, wout_a, g1, b1, tm=rows)
    y_s = y_s.reshape(rows, 1, D_MODEL)
    new_shape = (1, rows, 1, N_HEADS, HEAD_DIM)
    return (y_p, y_s, conv_p, conv_s, k_p[None], v_p[None],
            ks.reshape(new_shape), vs.reshape(new_shape))
```

```python
import functools

import numpy as np
import jax
import jax.numpy as jnp
from jax import lax
from jax.experimental import pallas as pl
from jax.experimental.pallas import tpu as pltpu

D_MODEL = 1024
DEPTH = 2
PAST_LEN = 2048
PAGE_SIZE = 128
N_HEADS = 8
HEAD_DIM = D_MODEL // N_HEADS
ROT_DIM = HEAD_DIM // 4
ROT_HALF = ROT_DIM // 2
ROPE_THETA = 500000.0
MOBA_BLOCK = 256
MOBA_TOP_K = 3
CONV_W = 3
DN_ALPHA = (2.0 * DEPTH) ** 0.25
LN_EPS = 1e-5
ATTN_SCALE = HEAD_DIM ** -0.5
LOG2E = 1.4426950408889634
PAGES_PER_BLOCK = MOBA_BLOCK // PAGE_SIZE

SUBLANES = 8
ROW_TILE = 512
COL_CHUNK = 512
SAMPLE_BLOCKS_PER_STEP = 4
VMEM_LIMIT = 56 * 1024 * 1024

BF16 = jnp.bfloat16
F32 = jnp.float32


def _dot(a, b):
    return jnp.dot(a, b, preferred_element_type=F32)


def _dot_nt(a, b):
    return lax.dot_general(a, b, (((1,), (1,)), ((), ())), preferred_element_type=F32)


def _layernorm(r, g, b):
    mu = jnp.mean(r, axis=-1, keepdims=True)
    c = r - mu
    var = jnp.mean(c * c, axis=-1, keepdims=True)
    return c * lax.rsqrt(var + LN_EPS) * g + b


def _silu(z):
    return z * jax.nn.sigmoid(z)


def _const_spec(shape):
    return pl.BlockSpec(shape, lambda *_: (0,) * len(shape), pipeline_mode=pl.Buffered(1))


def _conv_prompt_kernel(x_ref, win_ref, wconv_ref, wout_ref, g_ref, b_ref,
                        y_ref, state_ref, ext_ref, gate_ref, *, tm):
    t = pl.program_id(1)

    @pl.when(t == 0)
    def _():
        ext_ref[0:SUBLANES, :] = jnp.zeros((SUBLANES, D_MODEL), F32)

    x = x_ref[0]
    xb = x.astype(BF16)
    for c in range(D_MODEL // COL_CHUNK):
        lo = c * COL_CHUNK
        cols = slice(lo, lo + COL_CHUNK)
        b_gate = _dot(xb, win_ref[:, 0 * D_MODEL + lo:0 * D_MODEL + lo + COL_CHUNK])
        c_gate = _dot(xb, win_ref[:, 1 * D_MODEL + lo:1 * D_MODEL + lo + COL_CHUNK])
        h = _dot(xb, win_ref[:, 2 * D_MODEL + lo:2 * D_MODEL + lo + COL_CHUNK])
        z = _dot(xb, win_ref[:, 3 * D_MODEL + lo:3 * D_MODEL + lo + COL_CHUNK])
        u = c_gate * h
        ext_ref[SUBLANES:SUBLANES + tm, cols] = u
        ext = ext_ref[:, cols]
        u1 = pltpu.roll(ext, 1, 0)[SUBLANES:]
        u2 = pltpu.roll(ext, 2, 0)[SUBLANES:]
        conv = (wconv_ref[0:1, cols] * u2 + wconv_ref[1:2, cols] * u1
                + wconv_ref[2:3, cols] * u)
        gate_ref[:, cols] = (b_gate * conv * _silu(z)).astype(BF16)
    tail = ext_ref[tm:tm + SUBLANES, :]
    ext_ref[0:SUBLANES, :] = tail
    state_ref[0, 0] = tail
    y = _dot(gate_ref[...], wout_ref[...])
    y_ref[0] = _layernorm(DN_ALPHA * x + y, g_ref[...], b_ref[...])


def _conv_prompt(x, w_in, w_conv, w_out, g, b):
    bsz, seq, _ = x.shape
    tm = ROW_TILE
    assert seq % tm == 0
    return pl.pallas_call(
        functools.partial(_conv_prompt_kernel, tm=tm),
        grid=(bsz, seq // tm),
        in_specs=[
            pl.BlockSpec((1, tm, D_MODEL), lambda i, j: (i, j, 0)),
            _const_spec((D_MODEL, 4 * D_MODEL)),
            _const_spec((CONV_W, D_MODEL)),
            _const_spec((D_MODEL, D_MODEL)),
            _const_spec((1, D_MODEL)),
            _const_spec((1, D_MODEL)),
        ],
        out_specs=[
            pl.BlockSpec((1, tm, D_MODEL), lambda i, j: (i, j, 0)),
            pl.BlockSpec((1, 1, SUBLANES, D_MODEL), lambda i, j: (0, i, 0, 0)),
        ],
        out_shape=[
            jax.ShapeDtypeStruct((bsz, seq, D_MODEL), F32),
            jax.ShapeDtypeStruct((1, bsz, SUBLANES, D_MODEL), F32),
        ],
        scratch_shapes=[
            pltpu.VMEM((tm + SUBLANES, D_MODEL), F32),
            pltpu.VMEM((tm, D_MODEL), BF16),
        ],
        compiler_params=pltpu.CompilerParams(
            dimension_semantics=("arbitrary", "arbitrary"),
            vmem_limit_bytes=VMEM_LIMIT),
        name="conv_prompt",
    )(x, w_in, w_conv, w_out, g, b)


def _conv_sample_kernel(x_ref, prev_ref, win_ref, wconv_ref, wout_ref, g_ref, b_ref,
                        y_ref, state_ref):
    x = x_ref[...]
    xb = x.astype(BF16)
    b_gate = _dot(xb, win_ref[:, 0 * D_MODEL:1 * D_MODEL])
    c_gate = _dot(xb, win_ref[:, 1 * D_MODEL:2 * D_MODEL])
    h = _dot(xb, win_ref[:, 2 * D_MODEL:3 * D_MODEL])
    z = _dot(xb, win_ref[:, 3 * D_MODEL:4 * D_MODEL])
    u = c_gate * h
    prev0 = prev_ref[:, 0:D_MODEL]
    prev1 = prev_ref[:, D_MODEL:2 * D_MODEL]
    conv = wconv_ref[0:1, :] * prev0 + wconv_ref[1:2, :] * prev1 + wconv_ref[2:3, :] * u
    gate = (b_gate * conv * _silu(z)).astype(BF16)
    y = _dot(gate, wout_ref[...])
    y_ref[...] = _layernorm(DN_ALPHA * x + y, g_ref[...], b_ref[...])
    state_ref[:, 0:D_MODEL] = prev1
    state_ref[:, D_MODEL:2 * D_MODEL] = u


def _conv_sample(x, prev, w_in, w_conv, w_out, g, b):
    rows = x.shape[0]
    return pl.pallas_call(
        _conv_sample_kernel,
        out_shape=[
            jax.ShapeDtypeStruct((rows, D_MODEL), F32),
            jax.ShapeDtypeStruct((rows, 2 * D_MODEL), F32),
        ],
        compiler_params=pltpu.CompilerParams(vmem_limit_bytes=VMEM_LIMIT),
        name="conv_sample",
    )(x, prev, w_in, w_conv, w_out, g, b)


def _rope(x, cos_ref, sin_up_ref, sin_dn_ref):
    cos, sin_up, sin_dn = cos_ref[...], sin_up_ref[...], sin_dn_ref[...]
    heads = []
    for hd in range(N_HEADS):
        xh = x[:, hd * HEAD_DIM:(hd + 1) * HEAD_DIM]
        heads.append(xh * cos
                     + pltpu.roll(xh, ROT_HALF, 1) * sin_up
                     + pltpu.roll(xh, HEAD_DIM - ROT_HALF, 1) * sin_dn)
    return jnp.concatenate(heads, axis=1)


def _store_paged(ref, val):
    for pg in range(ref.shape[0]):
        for hd in range(N_HEADS):
            ref[pg, :, hd, :] = val[pg * PAGE_SIZE:(pg + 1) * PAGE_SIZE,
                                    hd * HEAD_DIM:(hd + 1) * HEAD_DIM].astype(ref.dtype)


def _attn_proj_prompt_kernel(x_ref, win_ref, cos_ref, sin_up_ref, sin_dn_ref,
                             q_ref, k_ref, v_ref, kb_ref, vb_ref, sz_ref, kmean_ref, *, tm):
    xb = x_ref[0].astype(BF16)
    q = _rope(_dot(xb, win_ref[:, 0 * D_MODEL:1 * D_MODEL]), cos_ref, sin_up_ref, sin_dn_ref)
    q_ref[0] = (q * (ATTN_SCALE * LOG2E)).astype(BF16)
    k = _rope(_dot(xb, win_ref[:, 1 * D_MODEL:2 * D_MODEL]), cos_ref, sin_up_ref, sin_dn_ref)
    _store_paged(k_ref.at[0], k)
    kb_ref[0] = k.astype(BF16)
    t = pl.program_id(1)
    for j in range(tm // MOBA_BLOCK):
        blk = k[j * MOBA_BLOCK:(j + 1) * MOBA_BLOCK]
        kmean_ref[0, pl.ds(t * (tm // MOBA_BLOCK) + j, 1), :] = (
            jnp.sum(blk, axis=0, keepdims=True) * (1.0 / MOBA_BLOCK))
    v = _dot(xb, win_ref[:, 2 * D_MODEL:3 * D_MODEL])
    _store_paged(v_ref.at[0], v)
    vb_ref[0] = v.astype(BF16)
    z = _dot(xb, win_ref[:, 3 * D_MODEL:4 * D_MODEL])
    sz_ref[0] = _silu(z).astype(BF16)


def _paged_spec(tm):
    return pl.BlockSpec((1, tm // PAGE_SIZE, PAGE_SIZE, N_HEADS, HEAD_DIM),
                        lambda i, j: (i, j, 0, 0, 0))


def _attn_proj_prompt(x, w_in, cos, sin_up, sin_dn):
    bsz, seq, _ = x.shape
    tm = ROW_TILE
    assert seq % tm == 0 and tm % MOBA_BLOCK == 0 and seq // MOBA_BLOCK == SUBLANES
    row_spec = pl.BlockSpec((1, tm, D_MODEL), lambda i, j: (i, j, 0))
    tab_spec = pl.BlockSpec((tm, HEAD_DIM), lambda i, j: (j, 0))
    flat = lambda dt: jax.ShapeDtypeStruct((bsz, seq, D_MODEL), dt)
    paged = jax.ShapeDtypeStruct((bsz, seq // PAGE_SIZE, PAGE_SIZE, N_HEADS, HEAD_DIM), F32)
    return pl.pallas_call(
        functools.partial(_attn_proj_prompt_kernel, tm=tm),
        grid=(bsz, seq // tm),
        in_specs=[row_spec, _const_spec((D_MODEL, 4 * D_MODEL)),
                  tab_spec, tab_spec, tab_spec],
        out_specs=[row_spec, _paged_spec(tm), _paged_spec(tm), row_spec, row_spec, row_spec,
                   pl.BlockSpec((1, SUBLANES, D_MODEL), lambda i, j: (i, 0, 0))],
        out_shape=[flat(BF16), paged, paged, flat(BF16), flat(BF16), flat(BF16),
                   jax.ShapeDtypeStruct((bsz, SUBLANES, D_MODEL), F32)],
        compiler_params=pltpu.CompilerParams(
            dimension_semantics=("arbitrary", "arbitrary"),
            vmem_limit_bytes=VMEM_LIMIT),
        name="attn_proj_prompt",
    )(x, w_in, cos, sin_up, sin_dn)


def _attn_proj_sample_kernel(x_ref, win_ref, cos_ref, sin_up_ref, sin_dn_ref,
                             q_ref, k_ref, v_ref, sz_ref):
    xb = x_ref[...].astype(BF16)
    q = _rope(_dot(xb, win_ref[:, 0 * D_MODEL:1 * D_MODEL]), cos_ref, sin_up_ref, sin_dn_ref)
    _store_paged(q_ref, q * (ATTN_SCALE * LOG2E))
    k = _rope(_dot(xb, win_ref[:, 1 * D_MODEL:2 * D_MODEL]), cos_ref, sin_up_ref, sin_dn_ref)
    _store_paged(k_ref, k)
    _store_paged(v_ref, _dot(xb, win_ref[:, 2 * D_MODEL:3 * D_MODEL]))
    _store_paged(sz_ref, _silu(_dot(xb, win_ref[:, 3 * D_MODEL:4 * D_MODEL])))


def _attn_proj_sample(x, w_in, cos, sin_up, sin_dn):
    rows = x.shape[0]
    assert rows % PAGE_SIZE == 0
    paged = jax.ShapeDtypeStruct((rows // PAGE_SIZE, PAGE_SIZE, N_HEADS, HEAD_DIM), F32)
    return pl.pallas_call(
        _attn_proj_sample_kernel,
        out_shape=[paged, paged, paged, paged],
        compiler_params=pltpu.CompilerParams(vmem_limit_bytes=VMEM_LIMIT),
        name="attn_proj_sample",
    )(x, w_in, cos, sin_up, sin_dn)


def _moba_prompt_kernel(q_ref, k_ref, v_ref, kmean_ref, sz_ref, o_ref, *, n_blocks):
    blk = MOBA_BLOCK
    kmean = kmean_ref[0].astype(BF16)
    v_t = v_ref[0].astype(F32).T.astype(BF16)
    blk_id = lax.broadcasted_iota(jnp.int32, (n_blocks, blk), 0)
    key = lax.broadcasted_iota(jnp.int32, (blk, blk), 0)
    qry = lax.broadcasted_iota(jnp.int32, (blk, blk), 1)
    causal = key <= qry
    for i in range(n_blocks):
        qi = q_ref[0, i * blk:(i + 1) * blk, :]
        scores = []
        if i > MOBA_TOP_K:
            gate = _dot_nt(kmean, qi)
        for n in range(i):
            s = _dot_nt(k_ref[0, n * blk:(n + 1) * blk, :], qi)
            if i > MOBA_TOP_K:
                gn = gate[n:n + 1, :]
                beats = ((gate > gn) | ((gate == gn) & (blk_id < n))) & (blk_id < i)
                rank = jnp.sum(beats.astype(F32), axis=0, keepdims=True)
                s = jnp.where(rank < MOBA_TOP_K, s, -jnp.inf)
            scores.append(s)
        s_own = _dot_nt(k_ref[0, i * blk:(i + 1) * blk, :], qi)
        scores.append(jnp.where(causal, s_own, -jnp.inf))
        m = scores[0].max(axis=0, keepdims=True)
        for s in scores[1:]:
            m = jnp.maximum(m, s.max(axis=0, keepdims=True))
        acc = jnp.zeros((HEAD_DIM, blk), F32)
        denom = jnp.zeros((1, blk), F32)
        for n, s in enumerate(scores):
            p = jnp.exp2(s - m)
            denom = denom + jnp.sum(p, axis=0, keepdims=True)
            acc = acc + _dot(v_t[:, n * blk:(n + 1) * blk], p.astype(BF16))
        o = (acc / denom).T
        o_ref[0, i * blk:(i + 1) * blk, :] = (
            o * sz_ref[0, i * blk:(i + 1) * blk, :].astype(F32)).astype(BF16)


def _moba_prompt(q, k, v, kmean, sz):
    bsz, seq, _ = q.shape
    n_blocks = seq // MOBA_BLOCK
    assert n_blocks == SUBLANES
    head_spec = pl.BlockSpec((1, seq, HEAD_DIM), lambda i, j: (i, 0, j))
    return pl.pallas_call(
        functools.partial(_moba_prompt_kernel, n_blocks=n_blocks),
        grid=(bsz, N_HEADS),
        in_specs=[head_spec, head_spec, head_spec,
                  pl.BlockSpec((1, n_blocks, HEAD_DIM), lambda i, j: (i, 0, j)),
                  head_spec],
        out_specs=head_spec,
        out_shape=jax.ShapeDtypeStruct((bsz, seq, D_MODEL), BF16),
        compiler_params=pltpu.CompilerParams(
            dimension_semantics=("arbitrary", "arbitrary"),
            vmem_limit_bytes=VMEM_LIMIT),
        name="moba_prompt",
    )(q, k, v, kmean, sz)


def _sum_tokens(x):
    t = x.shape[0]
    parts = jnp.sum(x.reshape((SUBLANES, t // SUBLANES) + x.shape[1:]), axis=1)
    return jnp.sum(parts, axis=0)


def _max_tokens(x):
    t = x.shape[0]
    parts = jnp.max(x.reshape((SUBLANES, t // SUBLANES) + x.shape[1:]), axis=1)
    return jnp.max(parts, axis=0)


def _moba_sample_kernel(pt_ref, q_ref, kn_ref, vn_ref, sz_ref, *refs, n_blocks, blocks_per_step):
    n_pages = blocks_per_step * PAGES_PER_BLOCK
    k_refs, v_refs = refs[:n_pages], refs[n_pages:2 * n_pages]
    o_ref, m_ref, l_ref, acc_ref, gate_ref = refs[2 * n_pages:]
    step = pl.program_id(1)
    tile = (N_HEADS, HEAD_DIM)
    q = q_ref[0]
    for j in range(blocks_per_step):
        pages = range(j * PAGES_PER_BLOCK, (j + 1) * PAGES_PER_BLOCK)
        scores = [jnp.sum(k_refs[pg][...] * q[None], axis=-1, keepdims=True) for pg in pages]
        m = functools.reduce(jnp.maximum, [_max_tokens(s) for s in scores])
        probs = [jnp.exp2(s - m[None]) for s in scores]
        l = sum(_sum_tokens(p) for p in probs)
        acc = sum(_sum_tokens(p * v_refs[pg][...]) for p, pg in zip(probs, pages))
        kmean = sum(_sum_tokens(k_refs[pg][...]) for pg in pages) * (1.0 / MOBA_BLOCK)
        n = step * blocks_per_step + j
        m_ref[n] = jnp.broadcast_to(m, tile)
        l_ref[n] = jnp.broadcast_to(l, tile)
        acc_ref[n] = acc
        gate_ref[n] = jnp.broadcast_to(jnp.sum(q * kmean, axis=-1, keepdims=True), tile)

    @pl.when(step == n_blocks // blocks_per_step - 1)
    def _():
        gates = gate_ref[...]
        blk_id = lax.broadcasted_iota(jnp.int32, (n_blocks,) + tile, 0)
        sel = jnp.zeros((n_blocks,) + tile, jnp.bool_)
        for j in range(n_blocks):
            gj = gates[j:j + 1]
            beats = (gates > gj) | ((gates == gj) & (blk_id < j))
            rank = jnp.sum(beats.astype(F32), axis=0, keepdims=True)
            sel = sel | ((blk_id == j) & (rank < MOBA_TOP_K))
        s_own = jnp.broadcast_to(jnp.sum(q * kn_ref[0], axis=-1, keepdims=True), tile)
        ms = jnp.where(sel, m_ref[...], -jnp.inf)
        m_fin = jnp.maximum(jnp.max(ms, axis=0), s_own)
        w = jnp.exp2(ms - m_fin[None])
        w_own = jnp.exp2(s_own - m_fin)
        denom = jnp.sum(w * l_ref[...], axis=0) + w_own
        numer = jnp.sum(w * acc_ref[...], axis=0) + w_own * vn_ref[0]
        o_ref[0] = (numer / denom) * sz_ref[0]


def _moba_sample(page_table, q, k_new, v_new, sz, cache_k, cache_v):
    rows = q.shape[0]
    n_blocks = PAST_LEN // MOBA_BLOCK
    blocks_per_step = SAMPLE_BLOCKS_PER_STEP
    n_pages = blocks_per_step * PAGES_PER_BLOCK
    assert PAST_LEN % MOBA_BLOCK == 0 and n_blocks % blocks_per_step == 0
    assert page_table.shape == (rows, n_blocks * PAGES_PER_BLOCK)
    assert cache_k.shape[0] == 1 and cache_k.shape[2:] == (PAGE_SIZE, N_HEADS, HEAD_DIM)
    row_spec = pl.BlockSpec((1, N_HEADS, HEAD_DIM), lambda i, j, pt: (i, 0, 0))

    def page(pg):
        return pl.BlockSpec((None, None, PAGE_SIZE, N_HEADS, HEAD_DIM),
                            lambda i, j, pt: (0, pt[i, n_pages * j + pg], 0, 0, 0))

    pages = [page(pg) for pg in range(n_pages)]
    stat = pltpu.VMEM((n_blocks, N_HEADS, HEAD_DIM), F32)
    return pl.pallas_call(
        functools.partial(_moba_sample_kernel, n_blocks=n_blocks, blocks_per_step=blocks_per_step),
        grid_spec=pltpu.PrefetchScalarGridSpec(
            num_scalar_prefetch=1,
            grid=(rows, n_blocks // blocks_per_step),
            in_specs=[row_spec, row_spec, row_spec, row_spec] + pages + pages,
            out_specs=row_spec,
            scratch_shapes=[stat, stat, stat, stat],
        ),
        out_shape=jax.ShapeDtypeStruct((rows, N_HEADS, HEAD_DIM), F32),
        compiler_params=pltpu.CompilerParams(
            dimension_semantics=("arbitrary", "arbitrary"),
            vmem_limit_bytes=VMEM_LIMIT),
        name="moba_sample",
    )(page_table, q, k_new, v_new, sz, *([cache_k] * n_pages), *([cache_v] * n_pages))


def _out_proj_kernel(a_ref, x_ref, w_ref, g_ref, b_ref, y_ref):
    y = _dot(a_ref[...].astype(BF16), w_ref[...])
    y_ref[...] = _layernorm(DN_ALPHA * x_ref[...] + y, g_ref[...], b_ref[...])


def _out_proj(a, x, w, g, b, *, tm):
    rows = x.shape[0]
    assert rows % tm == 0
    row_spec = pl.BlockSpec((tm, D_MODEL), lambda i: (i, 0))
    return pl.pallas_call(
        _out_proj_kernel,
        grid=(rows // tm,),
        in_specs=[row_spec, row_spec, _const_spec((D_MODEL, D_MODEL)),
                  _const_spec((1, D_MODEL)), _const_spec((1, D_MODEL))],
        out_specs=row_spec,
        out_shape=jax.ShapeDtypeStruct((rows, D_MODEL), F32),
        compiler_params=pltpu.CompilerParams(
            dimension_semantics=("arbitrary",), vmem_limit_bytes=VMEM_LIMIT),
        name="out_proj",
    )(a, x, w, g, b)


def _rope_tables(pos):
    inv = ROPE_THETA ** (-np.arange(ROT_HALF, dtype=np.float64) * 2.0 / ROT_DIM)
    ang = np.asarray(pos, np.float64)[:, None] * inv[None, :]
    cos, sin = np.cos(ang), np.sin(ang)
    n = ang.shape[0]
    rest = HEAD_DIM - ROT_DIM
    cos_t = np.concatenate([cos, cos, np.ones((n, rest))], axis=1)
    sin_up = np.concatenate([np.zeros((n, ROT_HALF)), sin, np.zeros((n, rest))], axis=1)
    sin_dn = np.concatenate([-sin, np.zeros((n, ROT_HALF + rest))], axis=1)
    return tuple(jnp.asarray(t, F32) for t in (cos_t, sin_up, sin_dn))


def kernel(x_prompt, x_sample, state_conv, cache_k, cache_v, page_table,
           w_in_conv, w_conv, w_out_conv, w_in_attn, w_out_attn, ln_g, ln_b):
    bsz, seq, _ = x_prompt.shape
    rows, dec_seq, _ = x_sample.shape
    assert dec_seq == 1 and w_in_conv.shape[0] == 1 and w_in_attn.shape[0] == 1
    win_c, wout_c = w_in_conv[0].astype(BF16), w_out_conv[0].astype(BF16)
    win_a, wout_a = w_in_attn[0].astype(BF16), w_out_attn[0].astype(BF16)
    g0, b0, g1, b1 = ln_g[0:1], ln_b[0:1], ln_g[1:2], ln_b[1:2]

    xp1, conv_tail = _conv_prompt(x_prompt, win_c, w_conv[0], wout_c, g0, b0)
    conv_p = conv_tail[:, :, SUBLANES - (CONV_W - 1):, :]
    xs1, conv_s = _conv_sample(
        x_sample.reshape(rows, D_MODEL), state_conv[0].reshape(rows, (CONV_W - 1) * D_MODEL),
        win_c, w_conv[0], wout_c, g0, b0)
    conv_s = conv_s.reshape(1, rows, CONV_W - 1, D_MODEL)

    q, k_p, v_p, kb, vb, sz, kmean = _attn_proj_prompt(xp1, win_a, *_rope_tables(np.arange(seq)))
    o = _moba_prompt(q, kb, vb, kmean, sz)
    y_p = _out_proj(o.reshape(bsz * seq, D_MODEL), xp1.reshape(bsz * seq, D_MODEL),
                    wout_a, g1, b1, tm=ROW_TILE)
    y_p = y_p.reshape(bsz, seq, D_MODEL)

    qs, ks, vs, szs = _attn_proj_sample(xs1, win_a, *_rope_tables(np.full((rows,), PAST_LEN)))
    per_row = lambda t: t.reshape(rows, N_HEADS, HEAD_DIM)
    os_ = _moba_sample(page_table, per_row(qs), per_row(ks), per_row(vs), per_row(szs),
                       cache_k, cache_v)
    y_s = _out_proj(os_.reshape(rows, D_MODEL), xs1, wout_a, g1, b1, tm=rows)
    y_s = y_s.reshape(rows, 1, D_MODEL)
    new_shape = (1, rows, 1, N_HEADS, HEAD_DIM)
    return (y_p, y_s, conv_p, conv_s, k_p[None], v_p[None],
            ks.reshape(new_shape), vs.reshape(new_shape))
```

```python
import functools
import itertools
from typing import NamedTuple

import numpy as np
import jax
import jax.numpy as jnp
from jax import lax
from jax.experimental import pallas as pl
from jax.experimental.pallas import tpu as pltpu

D_MODEL = 1024
DEPTH = 2
PAST_LEN = 2048
PAGE_SIZE = 128
N_HEADS = 8
HEAD_DIM = D_MODEL // N_HEADS
ROT_DIM = HEAD_DIM // 4
ROT_HALF = ROT_DIM // 2
ROPE_THETA = 500000.0
MOBA_BLOCK = 256
MOBA_TOP_K = 3
CONV_W = 3
DN_ALPHA = (2.0 * DEPTH) ** 0.25
LN_EPS = 1e-5
ATTN_SCALE = HEAD_DIM ** -0.5
LOG2E = 1.4426950408889634
PAGES_PER_BLOCK = MOBA_BLOCK // PAGE_SIZE

SUBLANES = 8
ROW_TILE = 256
OUT_ROW_TILE = 512
COL_CHUNK = 256
STREAM_CHUNK = 16
STREAM_SUM_PARTS = 2
VMEM_LIMIT = 56 * 1024 * 1024

BF16 = jnp.bfloat16
F32 = jnp.float32


def _dot(a, b):
    return jnp.dot(a, b, preferred_element_type=F32)


def _dot_nt(a, b):
    return lax.dot_general(a, b, (((1,), (1,)), ((), ())), preferred_element_type=F32)


def _layernorm(r, g, b):
    mu = jnp.mean(r, axis=-1, keepdims=True)
    c = r - mu
    var = jnp.mean(c * c, axis=-1, keepdims=True)
    return c * lax.rsqrt(var + LN_EPS) * g + b


def _silu(z):
    return z * jax.nn.sigmoid(z)


def _const_spec(shape):
    return pl.BlockSpec(shape, lambda *_: (0,) * len(shape), pipeline_mode=pl.Buffered(1))


def _conv_prompt_kernel(x_ref, win_ref, wconv_ref, wout_ref, g_ref, b_ref,
                        y_ref, state_ref, ext_ref, gate_ref, *, tm):
    t = pl.program_id(1)

    @pl.when(t == 0)
    def _():
        ext_ref[0:SUBLANES, :] = jnp.zeros((SUBLANES, D_MODEL), F32)

    x = x_ref[0]
    xb = x.astype(BF16)
    for c in range(D_MODEL // COL_CHUNK):
        lo = c * COL_CHUNK
        cols = slice(lo, lo + COL_CHUNK)
        b_gate = _dot(xb, win_ref[:, 0 * D_MODEL + lo:0 * D_MODEL + lo + COL_CHUNK])
        c_gate = _dot(xb, win_ref[:, 1 * D_MODEL + lo:1 * D_MODEL + lo + COL_CHUNK])
        h = _dot(xb, win_ref[:, 2 * D_MODEL + lo:2 * D_MODEL + lo + COL_CHUNK])
        z = _dot(xb, win_ref[:, 3 * D_MODEL + lo:3 * D_MODEL + lo + COL_CHUNK])
        u = c_gate * h
        ext_ref[SUBLANES:SUBLANES + tm, cols] = u
        ext = ext_ref[:, cols]
        u1 = pltpu.roll(ext, 1, 0)[SUBLANES:]
        u2 = pltpu.roll(ext, 2, 0)[SUBLANES:]
        conv = (wconv_ref[0:1, cols] * u2 + wconv_ref[1:2, cols] * u1
                + wconv_ref[2:3, cols] * u)
        gate_ref[:, cols] = (b_gate * conv * _silu(z)).astype(BF16)
        yield
    tail = ext_ref[tm:tm + SUBLANES, :]
    ext_ref[0:SUBLANES, :] = tail
    state_ref[0, 0] = tail
    y = _dot(gate_ref[...], wout_ref[...])
    y_ref[0] = _layernorm(DN_ALPHA * x + y, g_ref[...], b_ref[...])


def _conv_prompt(x, w_in, w_conv, w_out, g, b, stream):
    bsz, seq, _ = x.shape
    tm = ROW_TILE
    assert seq % tm == 0
    return _hosted_call(
        functools.partial(_conv_prompt_kernel, tm=tm), stream,
        name="conv_prompt",
        grid=(bsz, seq // tm),
        in_specs=[
            pl.BlockSpec((1, tm, D_MODEL), lambda i, j, *_: (i, j, 0)),
            _const_spec((D_MODEL, 4 * D_MODEL)),
            _const_spec((CONV_W, D_MODEL)),
            _const_spec((D_MODEL, D_MODEL)),
            _const_spec((1, D_MODEL)),
            _const_spec((1, D_MODEL)),
        ],
        out_specs=[
            pl.BlockSpec((1, tm, D_MODEL), lambda i, j, *_: (i, j, 0)),
            pl.BlockSpec((1, 1, SUBLANES, D_MODEL), lambda i, j, *_: (0, i, 0, 0)),
        ],
        out_shape=[
            jax.ShapeDtypeStruct((bsz, seq, D_MODEL), F32),
            jax.ShapeDtypeStruct((1, bsz, SUBLANES, D_MODEL), F32),
        ],
        scratch_shapes=[
            pltpu.VMEM((tm + SUBLANES, D_MODEL), F32),
            pltpu.VMEM((tm, D_MODEL), BF16),
        ],
        args=(x, w_in, w_conv, w_out, g, b),
    )


def _conv_sample_kernel(x_ref, prev_ref, win_ref, wconv_ref, wout_ref, g_ref, b_ref,
                        y_ref, state_ref):
    x = x_ref[...]
    xb = x.astype(BF16)
    b_gate = _dot(xb, win_ref[:, 0 * D_MODEL:1 * D_MODEL])
    c_gate = _dot(xb, win_ref[:, 1 * D_MODEL:2 * D_MODEL])
    h = _dot(xb, win_ref[:, 2 * D_MODEL:3 * D_MODEL])
    z = _dot(xb, win_ref[:, 3 * D_MODEL:4 * D_MODEL])
    u = c_gate * h
    prev0 = prev_ref[:, 0:D_MODEL]
    prev1 = prev_ref[:, D_MODEL:2 * D_MODEL]
    conv = wconv_ref[0:1, :] * prev0 + wconv_ref[1:2, :] * prev1 + wconv_ref[2:3, :] * u
    gate = (b_gate * conv * _silu(z)).astype(BF16)
    y = _dot(gate, wout_ref[...])
    y_ref[...] = _layernorm(DN_ALPHA * x + y, g_ref[...], b_ref[...])
    state_ref[:, 0:D_MODEL] = prev1
    state_ref[:, D_MODEL:2 * D_MODEL] = u


def _conv_sample(x, prev, w_in, w_conv, w_out, g, b):
    rows = x.shape[0]
    return pl.pallas_call(
        _conv_sample_kernel,
        out_shape=[
            jax.ShapeDtypeStruct((rows, D_MODEL), F32),
            jax.ShapeDtypeStruct((rows, 2 * D_MODEL), F32),
        ],
        compiler_params=pltpu.CompilerParams(vmem_limit_bytes=VMEM_LIMIT),
        name="conv_sample",
    )(x, prev, w_in, w_conv, w_out, g, b)


def _rope(x, cos_ref, sin_up_ref, sin_dn_ref):
    cos, sin_up, sin_dn = cos_ref[...], sin_up_ref[...], sin_dn_ref[...]
    heads = []
    for hd in range(N_HEADS):
        xh = x[:, hd * HEAD_DIM:(hd + 1) * HEAD_DIM]
        heads.append(xh * cos
                     + pltpu.roll(xh, ROT_HALF, 1) * sin_up
                     + pltpu.roll(xh, HEAD_DIM - ROT_HALF, 1) * sin_dn)
    return jnp.concatenate(heads, axis=1)


def _store_paged(ref, val):
    groups = PAGE_SIZE // SUBLANES
    sub = lax.broadcasted_iota(jnp.int32, (groups, SUBLANES, HEAD_DIM), 1)
    for pg in range(ref.shape[0]):
        slab = val[pg * PAGE_SIZE:(pg + 1) * PAGE_SIZE]
        parts = [slab[:, hd * HEAD_DIM:(hd + 1) * HEAD_DIM].reshape(groups, SUBLANES, HEAD_DIM)
                 for hd in range(N_HEADS)]
        dist = N_HEADS // 2
        while dist >= 1:
            keep = (sub & dist) == 0
            nxt = list(parts)
            for a in range(N_HEADS):
                if a & dist == 0:
                    b = a + dist
                    nxt[a] = jnp.where(keep, parts[a], pltpu.roll(parts[b], dist, 1))
                    nxt[b] = jnp.where(keep, pltpu.roll(parts[a], SUBLANES - dist, 1), parts[b])
            parts = nxt
            dist //= 2
        page = jnp.stack(parts, axis=1).reshape(PAGE_SIZE, N_HEADS, HEAD_DIM)
        ref[pg] = page.astype(ref.dtype)


def _attn_proj_prompt_kernel(x_ref, win_ref, cos_ref, sin_up_ref, sin_dn_ref,
                             q_ref, k_ref, v_ref, kb_ref, vb_ref, sz_ref, kmean_ref, *, tm):
    xb = x_ref[0].astype(BF16)
    q = _rope(_dot(xb, win_ref[:, 0 * D_MODEL:1 * D_MODEL]), cos_ref, sin_up_ref, sin_dn_ref)
    q_ref[0] = (q * (ATTN_SCALE * LOG2E)).astype(BF16)
    yield
    k = _rope(_dot(xb, win_ref[:, 1 * D_MODEL:2 * D_MODEL]), cos_ref, sin_up_ref, sin_dn_ref)
    _store_paged(k_ref.at[0], k)
    kb_ref[0] = k.astype(BF16)
    t = pl.program_id(1)
    for j in range(tm // MOBA_BLOCK):
        blk = k[j * MOBA_BLOCK:(j + 1) * MOBA_BLOCK]
        kmean_ref[0, pl.ds(t * (tm // MOBA_BLOCK) + j, 1), :] = (
            jnp.sum(blk, axis=0, keepdims=True) * (1.0 / MOBA_BLOCK))
    yield
    v = _dot(xb, win_ref[:, 2 * D_MODEL:3 * D_MODEL])
    _store_paged(v_ref.at[0], v)
    vb_ref[0] = v.astype(BF16)
    yield
    z = _dot(xb, win_ref[:, 3 * D_MODEL:4 * D_MODEL])
    sz_ref[0] = _silu(z).astype(BF16)


def _attn_proj_prompt(x, w_in, cos, sin_up, sin_dn, stream):
    bsz, seq, _ = x.shape
    tm = ROW_TILE
    assert seq % tm == 0 and tm % MOBA_BLOCK == 0 and seq // MOBA_BLOCK == SUBLANES
    row_spec = pl.BlockSpec((1, tm, D_MODEL), lambda i, j, *_: (i, j, 0))
    tab_spec = pl.BlockSpec((tm, HEAD_DIM), lambda i, j, *_: (j, 0))
    paged_spec = pl.BlockSpec((1, tm // PAGE_SIZE, PAGE_SIZE, N_HEADS, HEAD_DIM),
                              lambda i, j, *_: (i, j, 0, 0, 0))
    flat = lambda dt: jax.ShapeDtypeStruct((bsz, seq, D_MODEL), dt)
    paged = jax.ShapeDtypeStruct((bsz, seq // PAGE_SIZE, PAGE_SIZE, N_HEADS, HEAD_DIM), F32)
    return _hosted_call(
        functools.partial(_attn_proj_prompt_kernel, tm=tm), stream,
        name="attn_proj_prompt",
        grid=(bsz, seq // tm),
        in_specs=[row_spec, _const_spec((D_MODEL, 4 * D_MODEL)),
                  tab_spec, tab_spec, tab_spec],
        out_specs=[row_spec, paged_spec, paged_spec, row_spec, row_spec, row_spec,
                   pl.BlockSpec((1, SUBLANES, D_MODEL), lambda i, j, *_: (i, 0, 0))],
        out_shape=[flat(BF16), paged, paged, flat(BF16), flat(BF16), flat(BF16),
                   jax.ShapeDtypeStruct((bsz, SUBLANES, D_MODEL), F32)],
        scratch_shapes=[],
        args=(x, w_in, cos, sin_up, sin_dn),
    )


def _attn_proj_sample_kernel(x_ref, win_ref, cos_ref, sin_up_ref, sin_dn_ref,
                             q_ref, k_ref, v_ref, sz_ref):
    xb = x_ref[...].astype(BF16)
    q = _rope(_dot(xb, win_ref[:, 0 * D_MODEL:1 * D_MODEL]), cos_ref, sin_up_ref, sin_dn_ref)
    _store_paged(q_ref, q * (ATTN_SCALE * LOG2E))
    k = _rope(_dot(xb, win_ref[:, 1 * D_MODEL:2 * D_MODEL]), cos_ref, sin_up_ref, sin_dn_ref)
    _store_paged(k_ref, k)
    _store_paged(v_ref, _dot(xb, win_ref[:, 2 * D_MODEL:3 * D_MODEL]))
    _store_paged(sz_ref, _silu(_dot(xb, win_ref[:, 3 * D_MODEL:4 * D_MODEL])))


def _attn_proj_sample(x, w_in, cos, sin_up, sin_dn):
    rows = x.shape[0]
    assert rows % PAGE_SIZE == 0
    paged = jax.ShapeDtypeStruct((rows // PAGE_SIZE, PAGE_SIZE, N_HEADS, HEAD_DIM), F32)
    return pl.pallas_call(
        _attn_proj_sample_kernel,
        out_shape=[paged, paged, paged, paged],
        compiler_params=pltpu.CompilerParams(vmem_limit_bytes=VMEM_LIMIT),
        name="attn_proj_sample",
    )(x, w_in, cos, sin_up, sin_dn)


def _moba_prompt_kernel(q_ref, k_ref, v_ref, kmean_ref, sz_ref, o_ref, *, n_blocks):
    blk = MOBA_BLOCK
    kmean = kmean_ref[0].astype(BF16)
    v_t = v_ref[0].astype(F32).T.astype(BF16)
    blk_id = lax.broadcasted_iota(jnp.int32, (n_blocks, blk), 0)
    key = lax.broadcasted_iota(jnp.int32, (blk, blk), 0)
    qry = lax.broadcasted_iota(jnp.int32, (blk, blk), 1)
    causal = key <= qry
    for i in range(n_blocks):
        qi = q_ref[0, i * blk:(i + 1) * blk, :]
        scores = []
        if i > MOBA_TOP_K:
            gate = _dot_nt(kmean, qi)
        for n in range(i):
            s = _dot_nt(k_ref[0, n * blk:(n + 1) * blk, :], qi)
            if i > MOBA_TOP_K:
                gn = gate[n:n + 1, :]
                beats = ((gate > gn) | ((gate == gn) & (blk_id < n))) & (blk_id < i)
                rank = jnp.sum(beats.astype(F32), axis=0, keepdims=True)
                s = jnp.where(rank < MOBA_TOP_K, s, -jnp.inf)
            scores.append(s)
        s_own = _dot_nt(k_ref[0, i * blk:(i + 1) * blk, :], qi)
        scores.append(jnp.where(causal, s_own, -jnp.inf))
        m = scores[0].max(axis=0, keepdims=True)
        for s in scores[1:]:
            m = jnp.maximum(m, s.max(axis=0, keepdims=True))
        acc = jnp.zeros((HEAD_DIM, blk), F32)
        denom = jnp.zeros((1, blk), F32)
        for n, s in enumerate(scores):
            p = jnp.exp2(s - m)
            denom = denom + jnp.sum(p, axis=0, keepdims=True)
            acc = acc + _dot(v_t[:, n * blk:(n + 1) * blk], p.astype(BF16))
        o = (acc / denom).T
        o_ref[0, i * blk:(i + 1) * blk, :] = (
            o * sz_ref[0, i * blk:(i + 1) * blk, :].astype(F32)).astype(BF16)
        yield


def _moba_prompt(q, k, v, kmean, sz, stream):
    bsz, seq, _ = q.shape
    n_blocks = seq // MOBA_BLOCK
    assert n_blocks == SUBLANES
    head_spec = pl.BlockSpec((1, seq, HEAD_DIM), lambda i, j, *_: (i, 0, j))
    return _hosted_call(
        functools.partial(_moba_prompt_kernel, n_blocks=n_blocks), stream,
        name="moba_prompt",
        grid=(bsz, N_HEADS),
        in_specs=[head_spec, head_spec, head_spec,
                  pl.BlockSpec((1, n_blocks, HEAD_DIM), lambda i, j, *_: (i, 0, j)),
                  head_spec],
        out_specs=[head_spec],
        out_shape=[jax.ShapeDtypeStruct((bsz, seq, D_MODEL), BF16)],
        scratch_shapes=[],
        args=(q, k, v, kmean, sz),
    )


def _sum_tokens(x):
    t = x.shape[0]
    parts = jnp.sum(x.reshape((STREAM_SUM_PARTS, t // STREAM_SUM_PARTS) + x.shape[1:]), axis=1)
    return jnp.sum(parts, axis=0)


def _stream_body(step, q_ref, kn_ref, vn_ref, sz_ref, k_refs, v_refs,
                 o_ref, m_ref, l_ref, acc_ref, gate_ref, *, n_blocks, blocks_per_step):
    tile = (N_HEADS, HEAD_DIM)
    q = q_ref[0]
    for j in range(blocks_per_step):
        m = jnp.full((N_HEADS, 1), -jnp.inf, F32)
        l = jnp.zeros((N_HEADS, 1), F32)
        acc = jnp.zeros(tile, F32)
        ksum = jnp.zeros(tile, F32)
        for pg in range(j * PAGES_PER_BLOCK, (j + 1) * PAGES_PER_BLOCK):
            for c in range(PAGE_SIZE // STREAM_CHUNK):
                tok = slice(c * STREAM_CHUNK, (c + 1) * STREAM_CHUNK)
                kc = k_refs[pg][tok]
                s = jnp.sum(kc * q[None], axis=-1, keepdims=True)
                m_new = jnp.maximum(m, jnp.max(s, axis=0))
                alpha = jnp.exp2(m - m_new)
                p = jnp.exp2(s - m_new[None])
                l = l * alpha + _sum_tokens(p)
                acc = acc * alpha + _sum_tokens(p * v_refs[pg][tok])
                ksum = ksum + _sum_tokens(kc)
                m = m_new
        kmean = ksum * (1.0 / MOBA_BLOCK)
        n = step * blocks_per_step + j
        m_ref[n] = jnp.broadcast_to(m, tile)
        l_ref[n] = jnp.broadcast_to(l, tile)
        acc_ref[n] = acc
        gate_ref[n] = jnp.broadcast_to(jnp.sum(q * kmean, axis=-1, keepdims=True), tile)
        yield

    @pl.when(step == n_blocks // blocks_per_step - 1)
    def _():
        gates = gate_ref[...]
        blk_id = lax.broadcasted_iota(jnp.int32, (n_blocks,) + tile, 0)
        sel = jnp.zeros((n_blocks,) + tile, jnp.bool_)
        for j in range(n_blocks):
            gj = gates[j:j + 1]
            beats = (gates > gj) | ((gates == gj) & (blk_id < j))
            rank = jnp.sum(beats.astype(F32), axis=0, keepdims=True)
            sel = sel | ((blk_id == j) & (rank < MOBA_TOP_K))
        s_own = jnp.broadcast_to(jnp.sum(q * kn_ref[0], axis=-1, keepdims=True), tile)
        ms = jnp.where(sel, m_ref[...], -jnp.inf)
        m_fin = jnp.maximum(jnp.max(ms, axis=0), s_own)
        w = jnp.exp2(ms - m_fin[None])
        w_own = jnp.exp2(s_own - m_fin)
        denom = jnp.sum(w * l_ref[...], axis=0) + w_own
        numer = jnp.sum(w * acc_ref[...], axis=0) + w_own * vn_ref[0]
        o_ref[0] = (numer / denom) * sz_ref[0]


class _Stream(NamedTuple):
    page_table: jax.Array
    q: jax.Array
    k_new: jax.Array
    v_new: jax.Array
    sz: jax.Array
    cache_k: jax.Array
    cache_v: jax.Array
    first_seq: int
    n_seqs: int
    blocks_per_step: int


def _hosted_call(host_kernel, stream, *, name, grid, in_specs, out_specs, out_shape,
                 scratch_shapes, args):
    n_blocks = PAST_LEN // MOBA_BLOCK
    bps = stream.blocks_per_step
    groups = n_blocks // bps
    n_pages = bps * PAGES_PER_BLOCK
    g0, g1 = grid
    assert PAST_LEN % MOBA_BLOCK == 0 and n_blocks % bps == 0 and g0 * g1 == stream.n_seqs * groups
    assert stream.page_table.shape[1] == n_blocks * PAGES_PER_BLOCK
    assert stream.cache_k.shape[0] == 1 and stream.cache_k.shape[2:] == (PAGE_SIZE, N_HEADS, HEAD_DIM)

    n_steps = g0 * g1
    first_seq = stream.first_seq

    def seq_of(u):
        return u // groups

    def step_of(i, j):
        return i * g1 + j

    row_spec = pl.BlockSpec((1, N_HEADS, HEAD_DIM),
                            lambda i, j, pt: (first_seq + seq_of(step_of(i, j)), 0, 0))
    hbm_spec = pl.BlockSpec(memory_space=pl.ANY)
    page_buf = pltpu.VMEM((2, n_pages, PAGE_SIZE, N_HEADS, HEAD_DIM), F32)
    stat = pltpu.VMEM((n_blocks, N_HEADS, HEAD_DIM), F32)
    n_in, n_out, n_scr = len(in_specs), len(out_specs), len(scratch_shapes)

    def body(pt_ref, *refs):
        host_in, refs = refs[:n_in], refs[n_in:]
        rows, (ck_ref, cv_ref), refs = refs[:4], refs[4:6], refs[6:]
        host_out, o_ref, refs = refs[:n_out], refs[n_out], refs[n_out + 1:]
        host_scr, (kbuf, vbuf, sem), stats = refs[:n_scr], refs[n_scr:n_scr + 3], refs[n_scr + 3:]
        u = step_of(pl.program_id(0), pl.program_id(1))
        slot = u % 2

        def page_copies(step, to_slot):
            seq, first_page = first_seq + seq_of(step), (step % groups) * n_pages
            copies = []
            for pg in range(n_pages):
                page = pt_ref[seq, first_page + pg]
                copies.append(pltpu.make_async_copy(
                    ck_ref.at[0, page], kbuf.at[to_slot, pg], sem.at[to_slot, 0]))
                copies.append(pltpu.make_async_copy(
                    cv_ref.at[0, page], vbuf.at[to_slot, pg], sem.at[to_slot, 1]))
            return copies

        @pl.when(u == 0)
        def _():
            for c in page_copies(u, slot):
                c.start()

        @pl.when(u + 1 < n_steps)
        def _():
            for c in page_copies(u + 1, 1 - slot):
                c.start()

        for c in page_copies(u, slot):
            c.wait()

        host = host_kernel(*host_in, *host_out, *host_scr)
        k_refs = [kbuf.at[slot, pg] for pg in range(n_pages)]
        v_refs = [vbuf.at[slot, pg] for pg in range(n_pages)]
        strm = _stream_body(u % groups, *rows, k_refs, v_refs, o_ref, *stats,
                            n_blocks=n_blocks, blocks_per_step=bps)
        for _ in itertools.zip_longest(host, strm):
            pass

    res = pl.pallas_call(
        body,
        grid_spec=pltpu.PrefetchScalarGridSpec(
            num_scalar_prefetch=1,
            grid=grid,
            in_specs=list(in_specs) + [row_spec] * 4 + [hbm_spec, hbm_spec],
            out_specs=list(out_specs) + [
                pl.BlockSpec((1, N_HEADS, HEAD_DIM),
                             lambda i, j, pt: (seq_of(step_of(i, j)), 0, 0))],
            scratch_shapes=list(scratch_shapes) + [
                page_buf, page_buf, pltpu.SemaphoreType.DMA((2, 2))] + [stat] * 4,
        ),
        out_shape=list(out_shape) + [jax.ShapeDtypeStruct((stream.n_seqs, N_HEADS, HEAD_DIM), F32)],
        compiler_params=pltpu.CompilerParams(
            dimension_semantics=("arbitrary", "arbitrary"),
            vmem_limit_bytes=VMEM_LIMIT),
        name=name,
    )(stream.page_table, *args, stream.q, stream.k_new, stream.v_new, stream.sz,
      stream.cache_k, stream.cache_v)
    return res[:-1], res[-1]


def _out_proj_kernel(a_ref, x_ref, w_ref, g_ref, b_ref, y_ref):
    y = _dot(a_ref[...].astype(BF16), w_ref[...])
    y_ref[...] = _layernorm(DN_ALPHA * x_ref[...] + y, g_ref[...], b_ref[...])


def _out_proj(a, x, w, g, b, *, tm):
    rows = x.shape[0]
    assert rows % tm == 0
    row_spec = pl.BlockSpec((tm, D_MODEL), lambda i: (i, 0))
    return pl.pallas_call(
        _out_proj_kernel,
        grid=(rows // tm,),
        in_specs=[row_spec, row_spec, _const_spec((D_MODEL, D_MODEL)),
                  _const_spec((1, D_MODEL)), _const_spec((1, D_MODEL))],
        out_specs=row_spec,
        out_shape=jax.ShapeDtypeStruct((rows, D_MODEL), F32),
        compiler_params=pltpu.CompilerParams(
            dimension_semantics=("arbitrary",), vmem_limit_bytes=VMEM_LIMIT),
        name="out_proj",
    )(a, x, w, g, b)


def _rope_tables(pos):
    inv = ROPE_THETA ** (-np.arange(ROT_HALF, dtype=np.float64) * 2.0 / ROT_DIM)
    ang = np.asarray(pos, np.float64)[:, None] * inv[None, :]
    cos, sin = np.cos(ang), np.sin(ang)
    n = ang.shape[0]
    rest = HEAD_DIM - ROT_DIM
    cos_t = np.concatenate([cos, cos, np.ones((n, rest))], axis=1)
    sin_up = np.concatenate([np.zeros((n, ROT_HALF)), sin, np.zeros((n, rest))], axis=1)
    sin_dn = np.concatenate([-sin, np.zeros((n, ROT_HALF + rest))], axis=1)
    return tuple(jnp.asarray(t, F32) for t in (cos_t, sin_up, sin_dn))


def kernel(x_prompt, x_sample, state_conv, cache_k, cache_v, page_table,
           w_in_conv, w_conv, w_out_conv, w_in_attn, w_out_attn, ln_g, ln_b):
    bsz, seq, _ = x_prompt.shape
    rows, dec_seq, _ = x_sample.shape
    assert dec_seq == 1 and w_in_conv.shape[0] == 1 and w_in_attn.shape[0] == 1
    win_c, wout_c = w_in_conv[0].astype(BF16), w_out_conv[0].astype(BF16)
    win_a, wout_a = w_in_attn[0].astype(BF16), w_out_attn[0].astype(BF16)
    g0, b0, g1, b1 = ln_g[0:1], ln_b[0:1], ln_g[1:2], ln_b[1:2]

    xs1, conv_s = _conv_sample(
        x_sample.reshape(rows, D_MODEL), state_conv[0].reshape(rows, (CONV_W - 1) * D_MODEL),
        win_c, w_conv[0], wout_c, g0, b0)
    conv_s = conv_s.reshape(1, rows, CONV_W - 1, D_MODEL)
    qs, ks, vs, szs = _attn_proj_sample(xs1, win_a, *_rope_tables(np.full((rows,), PAST_LEN)))
    per_row = lambda t: t.reshape(rows, N_HEADS, HEAD_DIM)
    n_blocks = PAST_LEN // MOBA_BLOCK
    quarter = rows // 4

    def stream(first_seq, n_seqs, blocks_per_step):
        return _Stream(page_table, per_row(qs), per_row(ks), per_row(vs), per_row(szs),
                       cache_k, cache_v, first_seq, n_seqs, blocks_per_step)

    (xp1, conv_tail), o_a = _conv_prompt(x_prompt, win_c, w_conv[0], wout_c, g0, b0,
                                         stream(0, quarter, n_blocks // 2))
    conv_p = conv_tail[:, :, SUBLANES - (CONV_W - 1):, :]
    (q, k_p, v_p, kb, vb, sz, kmean), o_b = _attn_proj_prompt(
        xp1, win_a, *_rope_tables(np.arange(seq)), stream(quarter, quarter, n_blocks // 2))
    (o,), o_c = _moba_prompt(q, kb, vb, kmean, sz, stream(2 * quarter, 2 * quarter, n_blocks))
    y_p = _out_proj(o.reshape(bsz * seq, D_MODEL), xp1.reshape(bsz * seq, D_MODEL),
                    wout_a, g1, b1, tm=OUT_ROW_TILE)
    y_p = y_p.reshape(bsz, seq, D_MODEL)

    os_ = jnp.concatenate([o_a, o_b, o_c], axis=0)
    y_s = _out_proj(os_.reshape(rows, D_MODEL), xs1, wout_a, g1, b1, tm=rows)
    y_s = y_s.reshape(rows, 1, D_MODEL)
    new_shape = (1, rows, 1, N_HEADS, HEAD_DIM)
    return (y_p, y_s, conv_p, conv_s, k_p[None], v_p[None],
            ks.reshape(new_shape), vs.reshape(new_shape))
```

```python
import functools
import itertools
from typing import NamedTuple

import numpy as np
import jax
import jax.numpy as jnp
from jax import lax
from jax.experimental import pallas as pl
from jax.experimental.pallas import tpu as pltpu

D_MODEL = 1024
DEPTH = 2
PAST_LEN = 2048
PAGE_SIZE = 128
N_HEADS = 8
HEAD_DIM = D_MODEL // N_HEADS
ROT_DIM = HEAD_DIM // 4
ROT_HALF = ROT_DIM // 2
ROPE_THETA = 500000.0
MOBA_BLOCK = 256
MOBA_TOP_K = 3
CONV_W = 3
DN_ALPHA = (2.0 * DEPTH) ** 0.25
LN_EPS = 1e-5
ATTN_SCALE = HEAD_DIM ** -0.5
LOG2E = 1.4426950408889634
PAGES_PER_BLOCK = MOBA_BLOCK // PAGE_SIZE

SUBLANES = 8
BF16_SUBLANES = 16
ROW_TILE = 256
OUT_ROW_TILE = 512
COL_CHUNK = 256
STREAM_CHUNK = 16
STREAM_SUM_PARTS = 2
N_DMA_THREADS = 2
VMEM_LIMIT = 56 * 1024 * 1024

BF16 = jnp.bfloat16
F32 = jnp.float32


def _dot(a, b):
    return jnp.dot(a, b, preferred_element_type=F32)


def _dot_nt(a, b):
    return lax.dot_general(a, b, (((1,), (1,)), ((), ())), preferred_element_type=F32)


def _layernorm(r, g, b):
    mu = jnp.mean(r, axis=-1, keepdims=True)
    c = r - mu
    var = jnp.mean(c * c, axis=-1, keepdims=True)
    return c * lax.rsqrt(var + LN_EPS) * g + b


def _silu(z):
    return z * jax.nn.sigmoid(z)


def _const_spec(shape):
    return pl.BlockSpec(shape, lambda *_: (0,) * len(shape), pipeline_mode=pl.Buffered(1))


def _conv_prompt_kernel(x_ref, win_ref, wconv_ref, wout_ref, g_ref, b_ref,
                        y_ref, state_ref, ext_ref, gate_ref, *, tm):
    t = pl.program_id(1)

    @pl.when(t == 0)
    def _():
        ext_ref[0:SUBLANES, :] = jnp.zeros((SUBLANES, D_MODEL), F32)

    x = x_ref[0]
    xb = x.astype(BF16)
    for c in range(D_MODEL // COL_CHUNK):
        lo = c * COL_CHUNK
        cols = slice(lo, lo + COL_CHUNK)
        b_gate = _dot(xb, win_ref[:, 0 * D_MODEL + lo:0 * D_MODEL + lo + COL_CHUNK])
        c_gate = _dot(xb, win_ref[:, 1 * D_MODEL + lo:1 * D_MODEL + lo + COL_CHUNK])
        h = _dot(xb, win_ref[:, 2 * D_MODEL + lo:2 * D_MODEL + lo + COL_CHUNK])
        z = _dot(xb, win_ref[:, 3 * D_MODEL + lo:3 * D_MODEL + lo + COL_CHUNK])
        u = c_gate * h
        ext_ref[SUBLANES:SUBLANES + tm, cols] = u
        ext = ext_ref[:, cols]
        u1 = pltpu.roll(ext, 1, 0)[SUBLANES:]
        u2 = pltpu.roll(ext, 2, 0)[SUBLANES:]
        conv = (wconv_ref[0:1, cols] * u2 + wconv_ref[1:2, cols] * u1
                + wconv_ref[2:3, cols] * u)
        gate_ref[:, cols] = (b_gate * conv * _silu(z)).astype(BF16)
        yield
    tail = ext_ref[tm:tm + SUBLANES, :]
    ext_ref[0:SUBLANES, :] = tail
    state_ref[0, 0] = tail
    y = _dot(gate_ref[...], wout_ref[...])
    y_ref[0] = _layernorm(DN_ALPHA * x + y, g_ref[...], b_ref[...])


def _conv_prompt(x, w_in, w_conv, w_out, g, b, stream):
    bsz, seq, _ = x.shape
    tm = ROW_TILE
    assert seq % tm == 0
    return _hosted_call(
        functools.partial(_conv_prompt_kernel, tm=tm), stream,
        name="conv_prompt",
        grid=(bsz, seq // tm),
        in_specs=[
            pl.BlockSpec((1, tm, D_MODEL), lambda i, j, *_: (i, j, 0)),
            _const_spec((D_MODEL, 4 * D_MODEL)),
            _const_spec((CONV_W, D_MODEL)),
            _const_spec((D_MODEL, D_MODEL)),
            _const_spec((1, D_MODEL)),
            _const_spec((1, D_MODEL)),
        ],
        out_specs=[
            pl.BlockSpec((1, tm, D_MODEL), lambda i, j, *_: (i, j, 0)),
            pl.BlockSpec((1, 1, SUBLANES, D_MODEL), lambda i, j, *_: (0, i, 0, 0)),
        ],
        out_shape=[
            jax.ShapeDtypeStruct((bsz, seq, D_MODEL), F32),
            jax.ShapeDtypeStruct((1, bsz, SUBLANES, D_MODEL), F32),
        ],
        scratch_shapes=[
            pltpu.VMEM((tm + SUBLANES, D_MODEL), F32),
            pltpu.VMEM((tm, D_MODEL), BF16),
        ],
        args=(x, w_in, w_conv, w_out, g, b),
    )


def _conv_sample_kernel(x_ref, prev_ref, win_ref, wconv_ref, wout_ref, g_ref, b_ref,
                        y_ref, state_ref):
    x = x_ref[:, 0, :]
    xb = x.astype(BF16)
    b_gate = _dot(xb, win_ref[:, 0 * D_MODEL:1 * D_MODEL])
    c_gate = _dot(xb, win_ref[:, 1 * D_MODEL:2 * D_MODEL])
    h = _dot(xb, win_ref[:, 2 * D_MODEL:3 * D_MODEL])
    z = _dot(xb, win_ref[:, 3 * D_MODEL:4 * D_MODEL])
    u = c_gate * h
    prev0 = prev_ref[:, 0, :]
    prev1 = prev_ref[:, 1, :]
    conv = wconv_ref[0:1, :] * prev0 + wconv_ref[1:2, :] * prev1 + wconv_ref[2:3, :] * u
    gate = (b_gate * conv * _silu(z)).astype(BF16)
    y = _dot(gate, wout_ref[...])
    y_ref[...] = _layernorm(DN_ALPHA * x + y, g_ref[...], b_ref[...])
    state_ref[:, 0, :] = prev1
    state_ref[:, 1, :] = u


def _conv_sample(x, prev, w_in, w_conv, w_out, g, b):
    rows = x.shape[0]
    return pl.pallas_call(
        _conv_sample_kernel,
        out_shape=[
            jax.ShapeDtypeStruct((rows, D_MODEL), F32),
            jax.ShapeDtypeStruct((rows, CONV_W - 1, D_MODEL), F32),
        ],
        compiler_params=pltpu.CompilerParams(vmem_limit_bytes=VMEM_LIMIT),
        name="conv_sample",
    )(x, prev, w_in, w_conv, w_out, g, b)


def _rope(x, cos_ref, sin_up_ref, sin_dn_ref):
    cos, sin_up, sin_dn = cos_ref[...], sin_up_ref[...], sin_dn_ref[...]
    heads = []
    for hd in range(N_HEADS):
        xh = x[:, hd * HEAD_DIM:(hd + 1) * HEAD_DIM]
        heads.append(xh * cos
                     + pltpu.roll(xh, ROT_HALF, 1) * sin_up
                     + pltpu.roll(xh, HEAD_DIM - ROT_HALF, 1) * sin_dn)
    return jnp.concatenate(heads, axis=1)


def _store_paged(ref, val):
    groups = PAGE_SIZE // SUBLANES
    sub = lax.broadcasted_iota(jnp.int32, (groups, SUBLANES, HEAD_DIM), 1)
    for pg in range(ref.shape[0]):
        slab = val[pg * PAGE_SIZE:(pg + 1) * PAGE_SIZE]
        parts = [slab[:, hd * HEAD_DIM:(hd + 1) * HEAD_DIM].reshape(groups, SUBLANES, HEAD_DIM)
                 for hd in range(N_HEADS)]
        dist = N_HEADS // 2
        while dist >= 1:
            keep = (sub & dist) == 0
            nxt = list(parts)
            for a in range(N_HEADS):
                if a & dist == 0:
                    b = a + dist
                    nxt[a] = jnp.where(keep, parts[a], pltpu.roll(parts[b], dist, 1))
                    nxt[b] = jnp.where(keep, pltpu.roll(parts[a], SUBLANES - dist, 1), parts[b])
            parts = nxt
            dist //= 2
        page = jnp.stack(parts, axis=1).reshape(PAGE_SIZE, N_HEADS, HEAD_DIM)
        ref[pg] = page.astype(ref.dtype)


def _attn_proj_prompt_kernel(x_ref, win_ref, cos_ref, sin_up_ref, sin_dn_ref,
                             q_ref, k_ref, v_ref, kb_ref, vb_ref, sz_ref, kmean_ref, *, tm):
    xb = x_ref[0].astype(BF16)
    q = _rope(_dot(xb, win_ref[:, 0 * D_MODEL:1 * D_MODEL]), cos_ref, sin_up_ref, sin_dn_ref)
    q_ref[0] = (q * (ATTN_SCALE * LOG2E)).astype(BF16)
    yield
    k = _rope(_dot(xb, win_ref[:, 1 * D_MODEL:2 * D_MODEL]), cos_ref, sin_up_ref, sin_dn_ref)
    _store_paged(k_ref.at[0], k)
    kb_ref[0] = k.astype(BF16)
    t = pl.program_id(1)
    for j in range(tm // MOBA_BLOCK):
        blk = k[j * MOBA_BLOCK:(j + 1) * MOBA_BLOCK]
        kmean_ref[0, pl.ds(t * (tm // MOBA_BLOCK) + j, 1), :] = (
            jnp.sum(blk, axis=0, keepdims=True) * (1.0 / MOBA_BLOCK))
    yield
    v = _dot(xb, win_ref[:, 2 * D_MODEL:3 * D_MODEL])
    _store_paged(v_ref.at[0], v)
    vb_ref[0] = v.astype(BF16)
    yield
    z = _dot(xb, win_ref[:, 3 * D_MODEL:4 * D_MODEL])
    sz_ref[0] = _silu(z).astype(BF16)


def _attn_proj_prompt(x, w_in, cos, sin_up, sin_dn, stream):
    bsz, seq, _ = x.shape
    tm = ROW_TILE
    assert seq % tm == 0 and tm % MOBA_BLOCK == 0 and seq // MOBA_BLOCK == SUBLANES
    row_spec = pl.BlockSpec((1, tm, D_MODEL), lambda i, j, *_: (i, j, 0))
    tab_spec = pl.BlockSpec((tm, HEAD_DIM), lambda i, j, *_: (j, 0))
    paged_spec = pl.BlockSpec((1, tm // PAGE_SIZE, PAGE_SIZE, N_HEADS, HEAD_DIM),
                              lambda i, j, *_: (i, j, 0, 0, 0))
    flat = lambda dt: jax.ShapeDtypeStruct((bsz, seq, D_MODEL), dt)
    paged = jax.ShapeDtypeStruct((bsz, seq // PAGE_SIZE, PAGE_SIZE, N_HEADS, HEAD_DIM), F32)
    return _hosted_call(
        functools.partial(_attn_proj_prompt_kernel, tm=tm), stream,
        name="attn_proj_prompt",
        grid=(bsz, seq // tm),
        in_specs=[row_spec, _const_spec((D_MODEL, 4 * D_MODEL)),
                  tab_spec, tab_spec, tab_spec],
        out_specs=[row_spec, paged_spec, paged_spec, row_spec, row_spec, row_spec,
                   pl.BlockSpec((1, SUBLANES, D_MODEL), lambda i, j, *_: (i, 0, 0))],
        out_shape=[flat(BF16), paged, paged, flat(BF16), flat(BF16), flat(BF16),
                   jax.ShapeDtypeStruct((bsz, SUBLANES, D_MODEL), F32)],
        scratch_shapes=[],
        args=(x, w_in, cos, sin_up, sin_dn),
    )


def _attn_proj_sample_kernel(x_ref, win_ref, cos_ref, sin_up_ref, sin_dn_ref,
                             q_ref, k_ref, v_ref, sz_ref):
    xb = x_ref[...].astype(BF16)
    q = _rope(_dot(xb, win_ref[:, 0 * D_MODEL:1 * D_MODEL]), cos_ref, sin_up_ref, sin_dn_ref)
    _store_paged(q_ref, q * (ATTN_SCALE * LOG2E))
    k = _rope(_dot(xb, win_ref[:, 1 * D_MODEL:2 * D_MODEL]), cos_ref, sin_up_ref, sin_dn_ref)
    _store_paged(k_ref, k)
    _store_paged(v_ref, _dot(xb, win_ref[:, 2 * D_MODEL:3 * D_MODEL]))
    _store_paged(sz_ref, _silu(_dot(xb, win_ref[:, 3 * D_MODEL:4 * D_MODEL])))


def _attn_proj_sample(x, w_in, cos, sin_up, sin_dn):
    rows = x.shape[0]
    assert rows % PAGE_SIZE == 0
    paged = jax.ShapeDtypeStruct((rows // PAGE_SIZE, PAGE_SIZE, N_HEADS, HEAD_DIM), F32)
    return pl.pallas_call(
        _attn_proj_sample_kernel,
        out_shape=[paged, paged, paged, paged],
        compiler_params=pltpu.CompilerParams(vmem_limit_bytes=VMEM_LIMIT),
        name="attn_proj_sample",
    )(x, w_in, cos, sin_up, sin_dn)


def _moba_prompt_kernel(q_ref, k_ref, v_ref, kmean_ref, sz_ref, o_ref, *, n_blocks):
    blk = MOBA_BLOCK
    kmean = kmean_ref[0].astype(BF16)
    v_t = jnp.concatenate([v_ref[0].T,
                           jnp.ones((BF16_SUBLANES, v_ref.shape[1]), BF16)], axis=0)
    blk_id = lax.broadcasted_iota(jnp.int32, (n_blocks, blk), 0)
    key = lax.broadcasted_iota(jnp.int32, (blk, blk), 0)
    qry = lax.broadcasted_iota(jnp.int32, (blk, blk), 1)
    causal = key <= qry
    for i in range(n_blocks):
        qi = q_ref[0, i * blk:(i + 1) * blk, :]
        scores = []
        if i > MOBA_TOP_K:
            gate = _dot_nt(kmean, qi)
        for n in range(i):
            s = _dot_nt(k_ref[0, n * blk:(n + 1) * blk, :], qi)
            if i > MOBA_TOP_K:
                gn = gate[n:n + 1, :]
                beats = ((gate > gn) | ((gate == gn) & (blk_id < n))) & (blk_id < i)
                rank = jnp.sum(beats.astype(F32), axis=0, keepdims=True)
                s = jnp.where(rank < MOBA_TOP_K, s, -jnp.inf)
            scores.append(s)
        s_own = _dot_nt(k_ref[0, i * blk:(i + 1) * blk, :], qi)
        scores.append(jnp.where(causal, s_own, -jnp.inf))
        m = scores[0].max(axis=0, keepdims=True)
        for s in scores[1:]:
            m = jnp.maximum(m, s.max(axis=0, keepdims=True))
        acc = jnp.zeros((HEAD_DIM + BF16_SUBLANES, blk), F32)
        for n, s in enumerate(scores):
            p = jnp.exp2(s - m).astype(BF16)
            acc = acc + _dot(v_t[:, n * blk:(n + 1) * blk], p)
        o = (acc[:HEAD_DIM] / acc[HEAD_DIM:HEAD_DIM + 1]).T
        o_ref[0, i * blk:(i + 1) * blk, :] = (
            o * sz_ref[0, i * blk:(i + 1) * blk, :].astype(F32)).astype(BF16)
        yield


def _moba_prompt(q, k, v, kmean, sz, stream):
    bsz, seq, _ = q.shape
    n_blocks = seq // MOBA_BLOCK
    assert n_blocks == SUBLANES
    head_spec = pl.BlockSpec((1, seq, HEAD_DIM), lambda i, j, *_: (i, 0, j))
    return _hosted_call(
        functools.partial(_moba_prompt_kernel, n_blocks=n_blocks), stream,
        name="moba_prompt",
        grid=(bsz, N_HEADS),
        in_specs=[head_spec, head_spec, head_spec,
                  pl.BlockSpec((1, n_blocks, HEAD_DIM), lambda i, j, *_: (i, 0, j)),
                  head_spec],
        out_specs=[head_spec],
        out_shape=[jax.ShapeDtypeStruct((bsz, seq, D_MODEL), BF16)],
        scratch_shapes=[],
        args=(q, k, v, kmean, sz),
    )


def _sum_tokens(x):
    t = x.shape[0]
    parts = jnp.sum(x.reshape((STREAM_SUM_PARTS, t // STREAM_SUM_PARTS) + x.shape[1:]), axis=1)
    return jnp.sum(parts, axis=0)


def _stream_body(step, q_ref, kn_ref, vn_ref, sz_ref, k_refs, v_refs,
                 o_ref, m_ref, l_ref, acc_ref, gate_ref, *, n_blocks, blocks_per_step):
    tile = (N_HEADS, HEAD_DIM)
    q = q_ref[0]
    for j in range(blocks_per_step):
        m = jnp.full((N_HEADS, 1), -jnp.inf, F32)
        l = jnp.zeros((N_HEADS, 1), F32)
        acc = jnp.zeros(tile, F32)
        ksum = jnp.zeros(tile, F32)
        for pg in range(j * PAGES_PER_BLOCK, (j + 1) * PAGES_PER_BLOCK):
            for c in range(PAGE_SIZE // STREAM_CHUNK):
                tok = slice(c * STREAM_CHUNK, (c + 1) * STREAM_CHUNK)
                kc = k_refs[pg][tok]
                s = jnp.sum(kc * q[None], axis=-1, keepdims=True)
                m_new = jnp.maximum(m, jnp.max(s, axis=0))
                alpha = jnp.exp2(m - m_new)
                p = jnp.exp2(s - m_new[None])
                l = l * alpha + _sum_tokens(p)
                acc = acc * alpha + _sum_tokens(p * v_refs[pg][tok])
                ksum = ksum + _sum_tokens(kc)
                m = m_new
        kmean = ksum * (1.0 / MOBA_BLOCK)
        n = step * blocks_per_step + j
        m_ref[n] = jnp.broadcast_to(m, tile)
        l_ref[n] = jnp.broadcast_to(l, tile)
        acc_ref[n] = acc
        gate_ref[n] = jnp.broadcast_to(jnp.sum(q * kmean, axis=-1, keepdims=True), tile)
        yield

    @pl.when(step == n_blocks // blocks_per_step - 1)
    def _():
        gates = gate_ref[...]
        blk_id = lax.broadcasted_iota(jnp.int32, (n_blocks,) + tile, 0)
        sel = jnp.zeros((n_blocks,) + tile, jnp.bool_)
        for j in range(n_blocks):
            gj = gates[j:j + 1]
            beats = (gates > gj) | ((gates == gj) & (blk_id < j))
            rank = jnp.sum(beats.astype(F32), axis=0, keepdims=True)
            sel = sel | ((blk_id == j) & (rank < MOBA_TOP_K))
        s_own = jnp.broadcast_to(jnp.sum(q * kn_ref[0], axis=-1, keepdims=True), tile)
        ms = jnp.where(sel, m_ref[...], -jnp.inf)
        m_fin = jnp.maximum(jnp.max(ms, axis=0), s_own)
        w = jnp.exp2(ms - m_fin[None])
        w_own = jnp.exp2(s_own - m_fin)
        denom = jnp.sum(w * l_ref[...], axis=0) + w_own
        numer = jnp.sum(w * acc_ref[...], axis=0) + w_own * vn_ref[0]
        o_ref[0] = (numer / denom) * sz_ref[0]


class _Stream(NamedTuple):
    page_table: jax.Array
    q: jax.Array
    k_new: jax.Array
    v_new: jax.Array
    sz: jax.Array
    cache_k: jax.Array
    cache_v: jax.Array
    first_seq: int
    n_seqs: int
    blocks_per_step: int


def _hosted_call(host_kernel, stream, *, name, grid, in_specs, out_specs, out_shape,
                 scratch_shapes, args):
    n_blocks = PAST_LEN // MOBA_BLOCK
    bps = stream.blocks_per_step
    groups = n_blocks // bps
    n_pages = bps * PAGES_PER_BLOCK
    g0, g1 = grid
    assert PAST_LEN % MOBA_BLOCK == 0 and n_blocks % bps == 0 and g0 * g1 == stream.n_seqs * groups
    assert stream.page_table.shape[1] == n_blocks * PAGES_PER_BLOCK
    assert stream.cache_k.shape[0] == 1 and stream.cache_k.shape[2:] == (PAGE_SIZE, N_HEADS, HEAD_DIM)

    n_steps = g0 * g1
    first_seq = stream.first_seq

    def seq_of(u):
        return u // groups

    def step_of(i, j):
        return i * g1 + j

    row_spec = pl.BlockSpec((1, N_HEADS, HEAD_DIM),
                            lambda i, j, pt: (first_seq + seq_of(step_of(i, j)), 0, 0))
    hbm_spec = pl.BlockSpec(memory_space=pl.ANY)
    page_buf = pltpu.VMEM((2, n_pages, PAGE_SIZE, N_HEADS, HEAD_DIM), F32)
    stat = pltpu.VMEM((n_blocks, N_HEADS, HEAD_DIM), F32)
    n_in, n_out, n_scr = len(in_specs), len(out_specs), len(scratch_shapes)

    def body(pt_ref, *refs):
        host_in, refs = refs[:n_in], refs[n_in:]
        rows, (ck_ref, cv_ref), refs = refs[:4], refs[4:6], refs[6:]
        host_out, o_ref, refs = refs[:n_out], refs[n_out], refs[n_out + 1:]
        host_scr, (kbuf, vbuf, sem), stats = refs[:n_scr], refs[n_scr:n_scr + 3], refs[n_scr + 3:]
        u = step_of(pl.program_id(0), pl.program_id(1))
        slot = u % 2

        def page_copies(step, to_slot):
            seq, first_page = first_seq + seq_of(step), (step % groups) * n_pages
            copies = []
            for pg in range(n_pages):
                page = pt_ref[seq, first_page + pg]
                copies.append(pltpu.make_async_copy(
                    ck_ref.at[0, page], kbuf.at[to_slot, pg], sem.at[to_slot, 0]))
                copies.append(pltpu.make_async_copy(
                    cv_ref.at[0, page], vbuf.at[to_slot, pg], sem.at[to_slot, 1]))
            return copies

        def start_all(copies):
            for n, c in enumerate(copies):
                c.start(priority=n % N_DMA_THREADS)

        @pl.when(u == 0)
        def _():
            start_all(page_copies(u, slot))

        @pl.when(u + 1 < n_steps)
        def _():
            start_all(page_copies(u + 1, 1 - slot))

        for c in page_copies(u, slot):
            c.wait()

        host = host_kernel(*host_in, *host_out, *host_scr)
        k_refs = [kbuf.at[slot, pg] for pg in range(n_pages)]
        v_refs = [vbuf.at[slot, pg] for pg in range(n_pages)]
        strm = _stream_body(u % groups, *rows, k_refs, v_refs, o_ref, *stats,
                            n_blocks=n_blocks, blocks_per_step=bps)
        for _ in itertools.zip_longest(host, strm):
            pass

    res = pl.pallas_call(
        body,
        grid_spec=pltpu.PrefetchScalarGridSpec(
            num_scalar_prefetch=1,
            grid=grid,
            in_specs=list(in_specs) + [row_spec] * 4 + [hbm_spec, hbm_spec],
            out_specs=list(out_specs) + [
                pl.BlockSpec((1, N_HEADS, HEAD_DIM),
                             lambda i, j, pt: (seq_of(step_of(i, j)), 0, 0))],
            scratch_shapes=list(scratch_shapes) + [
                page_buf, page_buf, pltpu.SemaphoreType.DMA((2, 2))] + [stat] * 4,
        ),
        out_shape=list(out_shape) + [jax.ShapeDtypeStruct((stream.n_seqs, N_HEADS, HEAD_DIM), F32)],
        compiler_params=pltpu.CompilerParams(
            dimension_semantics=("arbitrary", "arbitrary"),
            vmem_limit_bytes=VMEM_LIMIT),
        name=name,
    )(stream.page_table, *args, stream.q, stream.k_new, stream.v_new, stream.sz,
      stream.cache_k, stream.cache_v)
    return res[:-1], res[-1]


def _out_proj_kernel(a_ref, x_ref, w_ref, g_ref, b_ref, y_ref):
    y = _dot(a_ref[...].astype(BF16), w_ref[...])
    y_ref[...] = _layernorm(DN_ALPHA * x_ref[...] + y, g_ref[...], b_ref[...])


def _out_proj_host_kernel(a_ref, x_ref, w_ref, g_ref, b_ref, y_ref):
    _out_proj_kernel(a_ref, x_ref, w_ref, g_ref, b_ref, y_ref)
    yield


def _out_proj_prompt(a, x, w, g, b, stream):
    rows = x.shape[0]
    tm = OUT_ROW_TILE
    assert rows % tm == 0
    row_spec = pl.BlockSpec((tm, D_MODEL), lambda i, j, *_: (i, 0))
    return _hosted_call(
        _out_proj_host_kernel, stream,
        name="out_proj_prompt",
        grid=(rows // tm, 1),
        in_specs=[row_spec, row_spec, _const_spec((D_MODEL, D_MODEL)),
                  _const_spec((1, D_MODEL)), _const_spec((1, D_MODEL))],
        out_specs=[row_spec],
        out_shape=[jax.ShapeDtypeStruct((rows, D_MODEL), F32)],
        scratch_shapes=[],
        args=(a, x, w, g, b),
    )


def _out_proj(a, x, w, g, b, *, tm):
    rows = x.shape[0]
    assert rows % tm == 0
    row_spec = pl.BlockSpec((tm, D_MODEL), lambda i: (i, 0))
    return pl.pallas_call(
        _out_proj_kernel,
        grid=(rows // tm,),
        in_specs=[row_spec, row_spec, _const_spec((D_MODEL, D_MODEL)),
                  _const_spec((1, D_MODEL)), _const_spec((1, D_MODEL))],
        out_specs=row_spec,
        out_shape=jax.ShapeDtypeStruct((rows, D_MODEL), F32),
        compiler_params=pltpu.CompilerParams(
            dimension_semantics=("arbitrary",), vmem_limit_bytes=VMEM_LIMIT),
        name="out_proj",
    )(a, x, w, g, b)


def _rope_tables(pos):
    inv = ROPE_THETA ** (-np.arange(ROT_HALF, dtype=np.float64) * 2.0 / ROT_DIM)
    ang = np.asarray(pos, np.float64)[:, None] * inv[None, :]
    cos, sin = np.cos(ang), np.sin(ang)
    n = ang.shape[0]
    rest = HEAD_DIM - ROT_DIM
    cos_t = np.concatenate([cos, cos, np.ones((n, rest))], axis=1)
    sin_up = np.concatenate([np.zeros((n, ROT_HALF)), sin, np.zeros((n, rest))], axis=1)
    sin_dn = np.concatenate([-sin, np.zeros((n, ROT_HALF + rest))], axis=1)
    return tuple(jnp.asarray(t, F32) for t in (cos_t, sin_up, sin_dn))


def kernel(x_prompt, x_sample, state_conv, cache_k, cache_v, page_table,
           w_in_conv, w_conv, w_out_conv, w_in_attn, w_out_attn, ln_g, ln_b):
    bsz, seq, _ = x_prompt.shape
    rows, dec_seq, _ = x_sample.shape
    assert dec_seq == 1 and w_in_conv.shape[0] == 1 and w_in_attn.shape[0] == 1
    win_c, wout_c = w_in_conv[0].astype(BF16), w_out_conv[0].astype(BF16)
    win_a, wout_a = w_in_attn[0].astype(BF16), w_out_attn[0].astype(BF16)
    g0, b0, g1, b1 = ln_g[0:1], ln_b[0:1], ln_g[1:2], ln_b[1:2]

    assert CONV_W == 3
    xs1, conv_s = _conv_sample(x_sample, state_conv[0], win_c, w_conv[0], wout_c, g0, b0)
    conv_s = conv_s[None]
    qs, ks, vs, szs = _attn_proj_sample(xs1, win_a, *_rope_tables(np.full((rows,), PAST_LEN)))
    per_row = lambda t: t.reshape(rows, N_HEADS, HEAD_DIM)
    n_blocks = PAST_LEN // MOBA_BLOCK
    share = rows // 8

    def stream(first_seq, n_seqs, blocks_per_step):
        return _Stream(page_table, per_row(qs), per_row(ks), per_row(vs), per_row(szs),
                       cache_k, cache_v, first_seq, n_seqs, blocks_per_step)

    (xp1, conv_tail), o_a = _conv_prompt(x_prompt, win_c, w_conv[0], wout_c, g0, b0,
                                         stream(0, 2 * share, n_blocks // 2))
    conv_p = conv_tail[:, :, SUBLANES - (CONV_W - 1):, :]
    (q, k_p, v_p, kb, vb, sz, kmean), o_b = _attn_proj_prompt(
        xp1, win_a, *_rope_tables(np.arange(seq)), stream(2 * share, share, n_blocks // 4))
    (o,), o_c = _moba_prompt(q, kb, vb, kmean, sz, stream(3 * share, 4 * share, n_blocks))
    (y_p,), o_d = _out_proj_prompt(
        o.reshape(bsz * seq, D_MODEL), xp1.reshape(bsz * seq, D_MODEL), wout_a, g1, b1,
        stream(7 * share, share, n_blocks // 2))
    y_p = y_p.reshape(bsz, seq, D_MODEL)

    os_ = jnp.concatenate([o_a, o_b, o_c, o_d], axis=0)
    y_s = _out_proj(os_.reshape(rows, D_MODEL), xs1, wout_a, g1, b1, tm=rows)
    y_s = y_s.reshape(rows, 1, D_MODEL)
    new_shape = (1, rows, 1, N_HEADS, HEAD_DIM)
    return (y_p, y_s, conv_p, conv_s, k_p[None], v_p[None],
            ks.reshape(new_shape), vs.reshape(new_shape))
```

```python
import functools
import itertools
from typing import NamedTuple

import numpy as np
import jax
import jax.numpy as jnp
from jax import lax
from jax.experimental import pallas as pl
from jax.experimental.pallas import tpu as pltpu

D_MODEL = 1024
DEPTH = 2
PAST_LEN = 2048
PAGE_SIZE = 128
N_HEADS = 8
HEAD_DIM = D_MODEL // N_HEADS
ROT_DIM = HEAD_DIM // 4
ROT_HALF = ROT_DIM // 2
ROPE_THETA = 500000.0
MOBA_BLOCK = 256
MOBA_TOP_K = 3
CONV_W = 3
DN_ALPHA = (2.0 * DEPTH) ** 0.25
LN_EPS = 1e-5
ATTN_SCALE = HEAD_DIM ** -0.5
LOG2E = 1.4426950408889634
PAGES_PER_BLOCK = MOBA_BLOCK // PAGE_SIZE

SUBLANES = 8
BF16_SUBLANES = 16
ROW_TILE = 256
OUT_ROW_TILE = 1024
COL_CHUNK = 256
STREAM_CHUNK = 16
STREAM_SUM_PARTS = 2
N_DMA_THREADS = 2
VMEM_LIMIT = 56 * 1024 * 1024

BF16 = jnp.bfloat16
F32 = jnp.float32


def _dot(a, b):
    return jnp.dot(a, b, preferred_element_type=F32)


def _dot_nt(a, b):
    return lax.dot_general(a, b, (((1,), (1,)), ((), ())), preferred_element_type=F32)


def _layernorm(r, g, b):
    mu = jnp.mean(r, axis=-1, keepdims=True)
    c = r - mu
    var = jnp.mean(c * c, axis=-1, keepdims=True)
    return c * lax.rsqrt(var + LN_EPS) * g + b


def _silu(z):
    return z * jax.nn.sigmoid(z)


def _const_spec(shape):
    return pl.BlockSpec(shape, lambda *_: (0,) * len(shape), pipeline_mode=pl.Buffered(1))


def _conv_prompt_kernel(x_ref, win_ref, wconv_ref, wout_ref,
                        r_ref, state_ref, ext_ref, gate_ref, *, tm):
    t = pl.program_id(1)

    @pl.when(t == 0)
    def _():
        ext_ref[0:SUBLANES, :] = jnp.zeros((SUBLANES, D_MODEL), F32)

    x = x_ref[0]
    xb = x.astype(BF16)
    for c in range(D_MODEL // COL_CHUNK):
        lo = c * COL_CHUNK
        cols = slice(lo, lo + COL_CHUNK)
        b_gate = _dot(xb, win_ref[:, 0 * D_MODEL + lo:0 * D_MODEL + lo + COL_CHUNK])
        c_gate = _dot(xb, win_ref[:, 1 * D_MODEL + lo:1 * D_MODEL + lo + COL_CHUNK])
        h = _dot(xb, win_ref[:, 2 * D_MODEL + lo:2 * D_MODEL + lo + COL_CHUNK])
        z = _dot(xb, win_ref[:, 3 * D_MODEL + lo:3 * D_MODEL + lo + COL_CHUNK])
        u = c_gate * h
        ext_ref[SUBLANES:SUBLANES + tm, cols] = u
        ext = ext_ref[:, cols]
        u1 = pltpu.roll(ext, 1, 0)[SUBLANES:]
        u2 = pltpu.roll(ext, 2, 0)[SUBLANES:]
        conv = (wconv_ref[0:1, cols] * u2 + wconv_ref[1:2, cols] * u1
                + wconv_ref[2:3, cols] * u)
        gate_ref[:, cols] = (b_gate * conv * _silu(z)).astype(BF16)
        yield
    tail = ext_ref[tm:tm + SUBLANES, :]
    ext_ref[0:SUBLANES, :] = tail
    state_ref[0, 0] = tail
    r_ref[0] = DN_ALPHA * x + _dot(gate_ref[...], wout_ref[...])


def _conv_prompt(x, w_in, w_conv, w_out, stream):
    bsz, seq, _ = x.shape
    tm = ROW_TILE
    assert seq % tm == 0
    return _hosted_call(
        functools.partial(_conv_prompt_kernel, tm=tm), stream,
        name="conv_prompt",
        grid=(bsz, seq // tm),
        in_specs=[
            pl.BlockSpec((1, tm, D_MODEL), lambda i, j, *_: (i, j, 0)),
            _const_spec((D_MODEL, 4 * D_MODEL)),
            _const_spec((CONV_W, D_MODEL)),
            _const_spec((D_MODEL, D_MODEL)),
        ],
        out_specs=[
            pl.BlockSpec((1, tm, D_MODEL), lambda i, j, *_: (i, j, 0)),
            pl.BlockSpec((1, 1, SUBLANES, D_MODEL), lambda i, j, *_: (0, i, 0, 0)),
        ],
        out_shape=[
            jax.ShapeDtypeStruct((bsz, seq, D_MODEL), F32),
            jax.ShapeDtypeStruct((1, bsz, SUBLANES, D_MODEL), F32),
        ],
        scratch_shapes=[
            pltpu.VMEM((tm + SUBLANES, D_MODEL), F32),
            pltpu.VMEM((tm, D_MODEL), BF16),
        ],
        args=(x, w_in, w_conv, w_out),
    )


def _conv_sample_kernel(x_ref, prev_ref, win_ref, wconv_ref, wout_ref, g_ref, b_ref,
                        y_ref, state_ref):
    x = x_ref[:, 0, :]
    xb = x.astype(BF16)
    b_gate = _dot(xb, win_ref[:, 0 * D_MODEL:1 * D_MODEL])
    c_gate = _dot(xb, win_ref[:, 1 * D_MODEL:2 * D_MODEL])
    h = _dot(xb, win_ref[:, 2 * D_MODEL:3 * D_MODEL])
    z = _dot(xb, win_ref[:, 3 * D_MODEL:4 * D_MODEL])
    u = c_gate * h
    prev0 = prev_ref[:, 0, :]
    prev1 = prev_ref[:, 1, :]
    conv = wconv_ref[0:1, :] * prev0 + wconv_ref[1:2, :] * prev1 + wconv_ref[2:3, :] * u
    gate = (b_gate * conv * _silu(z)).astype(BF16)
    y = _dot(gate, wout_ref[...])
    y_ref[...] = _layernorm(DN_ALPHA * x + y, g_ref[...], b_ref[...])
    state_ref[:, 0, :] = prev1
    state_ref[:, 1, :] = u


def _conv_sample(x, prev, w_in, w_conv, w_out, g, b):
    rows = x.shape[0]
    return pl.pallas_call(
        _conv_sample_kernel,
        out_shape=[
            jax.ShapeDtypeStruct((rows, D_MODEL), F32),
            jax.ShapeDtypeStruct((rows, CONV_W - 1, D_MODEL), F32),
        ],
        compiler_params=pltpu.CompilerParams(vmem_limit_bytes=VMEM_LIMIT),
        name="conv_sample",
    )(x, prev, w_in, w_conv, w_out, g, b)


def _rope(x, cos_ref, sin_up_ref, sin_dn_ref):
    cos, sin_up, sin_dn = cos_ref[...], sin_up_ref[...], sin_dn_ref[...]
    heads = []
    for hd in range(N_HEADS):
        xh = x[:, hd * HEAD_DIM:(hd + 1) * HEAD_DIM]
        heads.append(xh * cos
                     + pltpu.roll(xh, ROT_HALF, 1) * sin_up
                     + pltpu.roll(xh, HEAD_DIM - ROT_HALF, 1) * sin_dn)
    return jnp.concatenate(heads, axis=1)


def _store_paged(ref, val):
    groups = PAGE_SIZE // SUBLANES
    sub = lax.broadcasted_iota(jnp.int32, (groups, SUBLANES, HEAD_DIM), 1)
    for pg in range(ref.shape[0]):
        slab = val[pg * PAGE_SIZE:(pg + 1) * PAGE_SIZE]
        parts = [slab[:, hd * HEAD_DIM:(hd + 1) * HEAD_DIM].reshape(groups, SUBLANES, HEAD_DIM)
                 for hd in range(N_HEADS)]
        dist = N_HEADS // 2
        while dist >= 1:
            keep = (sub & dist) == 0
            nxt = list(parts)
            for a in range(N_HEADS):
                if a & dist == 0:
                    b = a + dist
                    nxt[a] = jnp.where(keep, parts[a], pltpu.roll(parts[b], dist, 1))
                    nxt[b] = jnp.where(keep, pltpu.roll(parts[a], SUBLANES - dist, 1), parts[b])
            parts = nxt
            dist //= 2
        page = jnp.stack(parts, axis=1).reshape(PAGE_SIZE, N_HEADS, HEAD_DIM)
        ref[pg] = page.astype(ref.dtype)


def _attn_proj_prompt_kernel(r_ref, g_ref, b_ref, win_ref, cos_ref, sin_up_ref, sin_dn_ref,
                             q_ref, k_ref, v_ref, kb_ref, vb_ref, sz_ref, kmean_ref, *, tm):
    xb = _layernorm(r_ref[0], g_ref[...], b_ref[...]).astype(BF16)
    q = _rope(_dot(xb, win_ref[:, 0 * D_MODEL:1 * D_MODEL]), cos_ref, sin_up_ref, sin_dn_ref)
    q_ref[0] = (q * (ATTN_SCALE * LOG2E)).astype(BF16)
    yield
    k = _rope(_dot(xb, win_ref[:, 1 * D_MODEL:2 * D_MODEL]), cos_ref, sin_up_ref, sin_dn_ref)
    _store_paged(k_ref.at[0], k)
    kb_ref[0] = k.astype(BF16)
    t = pl.program_id(1)
    for j in range(tm // MOBA_BLOCK):
        blk = k[j * MOBA_BLOCK:(j + 1) * MOBA_BLOCK]
        kmean_ref[0, pl.ds(t * (tm // MOBA_BLOCK) + j, 1), :] = (
            jnp.sum(blk, axis=0, keepdims=True) * (1.0 / MOBA_BLOCK))
    yield
    v = _dot(xb, win_ref[:, 2 * D_MODEL:3 * D_MODEL])
    _store_paged(v_ref.at[0], v)
    vb_ref[0] = v.astype(BF16)
    yield
    z = _dot(xb, win_ref[:, 3 * D_MODEL:4 * D_MODEL])
    sz_ref[0] = _silu(z).astype(BF16)


def _attn_proj_prompt(r, g, b, w_in, cos, sin_up, sin_dn, stream):
    bsz, seq, _ = r.shape
    tm = ROW_TILE
    assert seq % tm == 0 and tm % MOBA_BLOCK == 0 and seq // MOBA_BLOCK == SUBLANES
    row_spec = pl.BlockSpec((1, tm, D_MODEL), lambda i, j, *_: (i, j, 0))
    tab_spec = pl.BlockSpec((tm, HEAD_DIM), lambda i, j, *_: (j, 0))
    paged_spec = pl.BlockSpec((1, tm // PAGE_SIZE, PAGE_SIZE, N_HEADS, HEAD_DIM),
                              lambda i, j, *_: (i, j, 0, 0, 0))
    flat = lambda dt: jax.ShapeDtypeStruct((bsz, seq, D_MODEL), dt)
    paged = jax.ShapeDtypeStruct((bsz, seq // PAGE_SIZE, PAGE_SIZE, N_HEADS, HEAD_DIM), F32)
    return _hosted_call(
        functools.partial(_attn_proj_prompt_kernel, tm=tm), stream,
        name="attn_proj_prompt",
        grid=(bsz, seq // tm),
        in_specs=[row_spec, _const_spec((1, D_MODEL)), _const_spec((1, D_MODEL)),
                  _const_spec((D_MODEL, 4 * D_MODEL)), tab_spec, tab_spec, tab_spec],
        out_specs=[row_spec, paged_spec, paged_spec, row_spec, row_spec, row_spec,
                   pl.BlockSpec((1, SUBLANES, D_MODEL), lambda i, j, *_: (i, 0, 0))],
        out_shape=[flat(BF16), paged, paged, flat(BF16), flat(BF16), flat(BF16),
                   jax.ShapeDtypeStruct((bsz, SUBLANES, D_MODEL), F32)],
        scratch_shapes=[],
        args=(r, g, b, w_in, cos, sin_up, sin_dn),
    )


def _attn_proj_sample_kernel(x_ref, win_ref, cos_ref, sin_up_ref, sin_dn_ref,
                             q_ref, k_ref, v_ref, sz_ref):
    xb = x_ref[...].astype(BF16)
    q = _rope(_dot(xb, win_ref[:, 0 * D_MODEL:1 * D_MODEL]), cos_ref, sin_up_ref, sin_dn_ref)
    _store_paged(q_ref, q * (ATTN_SCALE * LOG2E))
    k = _rope(_dot(xb, win_ref[:, 1 * D_MODEL:2 * D_MODEL]), cos_ref, sin_up_ref, sin_dn_ref)
    _store_paged(k_ref, k)
    _store_paged(v_ref, _dot(xb, win_ref[:, 2 * D_MODEL:3 * D_MODEL]))
    _store_paged(sz_ref, _silu(_dot(xb, win_ref[:, 3 * D_MODEL:4 * D_MODEL])))


def _attn_proj_sample(x, w_in, cos, sin_up, sin_dn):
    rows = x.shape[0]
    assert rows % PAGE_SIZE == 0
    paged = jax.ShapeDtypeStruct((rows // PAGE_SIZE, PAGE_SIZE, N_HEADS, HEAD_DIM), F32)
    return pl.pallas_call(
        _attn_proj_sample_kernel,
        out_shape=[paged, paged, paged, paged],
        compiler_params=pltpu.CompilerParams(vmem_limit_bytes=VMEM_LIMIT),
        name="attn_proj_sample",
    )(x, w_in, cos, sin_up, sin_dn)


def _moba_prompt_kernel(q_ref, k_ref, v_ref, kmean_ref, sz_ref, o_ref, *, n_blocks):
    blk = MOBA_BLOCK
    kmean = kmean_ref[0].astype(BF16)
    v_t = jnp.concatenate([v_ref[0].T,
                           jnp.ones((BF16_SUBLANES, v_ref.shape[1]), BF16)], axis=0)
    blk_id = lax.broadcasted_iota(jnp.int32, (n_blocks, blk), 0)
    key = lax.broadcasted_iota(jnp.int32, (blk, blk), 0)
    qry = lax.broadcasted_iota(jnp.int32, (blk, blk), 1)
    causal = key <= qry
    for i in range(n_blocks):
        qi = q_ref[0, i * blk:(i + 1) * blk, :]
        scores = []
        if i > MOBA_TOP_K:
            gate = _dot_nt(kmean, qi)
        for n in range(i):
            s = _dot_nt(k_ref[0, n * blk:(n + 1) * blk, :], qi)
            if i > MOBA_TOP_K:
                gn = gate[n:n + 1, :]
                beats = ((gate > gn) | ((gate == gn) & (blk_id < n))) & (blk_id < i)
                rank = jnp.sum(beats.astype(F32), axis=0, keepdims=True)
                s = jnp.where(rank < MOBA_TOP_K, s, -jnp.inf)
            scores.append(s)
        s_own = _dot_nt(k_ref[0, i * blk:(i + 1) * blk, :], qi)
        scores.append(jnp.where(causal, s_own, -jnp.inf))
        m = scores[0].max(axis=0, keepdims=True)
        for s in scores[1:]:
            m = jnp.maximum(m, s.max(axis=0, keepdims=True))
        acc = jnp.zeros((HEAD_DIM + BF16_SUBLANES, blk), F32)
        for n, s in enumerate(scores):
            p = jnp.exp2(s - m).astype(BF16)
            acc = acc + _dot(v_t[:, n * blk:(n + 1) * blk], p)
        o = (acc[:HEAD_DIM] / acc[HEAD_DIM:HEAD_DIM + 1]).T
        o_ref[0, i * blk:(i + 1) * blk, :] = (
            o * sz_ref[0, i * blk:(i + 1) * blk, :].astype(F32)).astype(BF16)
        yield


def _moba_prompt(q, k, v, kmean, sz, stream):
    bsz, seq, _ = q.shape
    n_blocks = seq // MOBA_BLOCK
    assert n_blocks == SUBLANES
    head_spec = pl.BlockSpec((1, seq, HEAD_DIM), lambda i, j, *_: (i, 0, j))
    return _hosted_call(
        functools.partial(_moba_prompt_kernel, n_blocks=n_blocks), stream,
        name="moba_prompt",
        grid=(bsz, N_HEADS),
        in_specs=[head_spec, head_spec, head_spec,
                  pl.BlockSpec((1, n_blocks, HEAD_DIM), lambda i, j, *_: (i, 0, j)),
                  head_spec],
        out_specs=[head_spec],
        out_shape=[jax.ShapeDtypeStruct((bsz, seq, D_MODEL), BF16)],
        scratch_shapes=[],
        args=(q, k, v, kmean, sz),
    )


def _sum_tokens(x):
    t = x.shape[0]
    parts = jnp.sum(x.reshape((STREAM_SUM_PARTS, t // STREAM_SUM_PARTS) + x.shape[1:]), axis=1)
    return jnp.sum(parts, axis=0)


def _stream_body(step, q_ref, kn_ref, vn_ref, sz_ref, k_refs, v_refs,
                 o_ref, m_ref, l_ref, acc_ref, gate_ref, *, n_blocks, blocks_per_step):
    tile = (N_HEADS, HEAD_DIM)
    q = q_ref[0]
    for j in range(blocks_per_step):
        m = jnp.full((N_HEADS, 1), -jnp.inf, F32)
        l = jnp.zeros((N_HEADS, 1), F32)
        acc = jnp.zeros(tile, F32)
        ksum = jnp.zeros(tile, F32)
        for pg in range(j * PAGES_PER_BLOCK, (j + 1) * PAGES_PER_BLOCK):
            for c in range(PAGE_SIZE // STREAM_CHUNK):
                tok = slice(c * STREAM_CHUNK, (c + 1) * STREAM_CHUNK)
                kc = k_refs[pg][tok]
                s = jnp.sum(kc * q[None], axis=-1, keepdims=True)
                m_new = jnp.maximum(m, jnp.max(s, axis=0))
                alpha = jnp.exp2(m - m_new)
                p = jnp.exp2(s - m_new[None])
                l = l * alpha + _sum_tokens(p)
                acc = acc * alpha + _sum_tokens(p * v_refs[pg][tok])
                ksum = ksum + _sum_tokens(kc)
                m = m_new
        kmean = ksum * (1.0 / MOBA_BLOCK)
        n = step * blocks_per_step + j
        m_ref[n] = jnp.broadcast_to(m, tile)
        l_ref[n] = jnp.broadcast_to(l, tile)
        acc_ref[n] = acc
        gate_ref[n] = jnp.broadcast_to(jnp.sum(q * kmean, axis=-1, keepdims=True), tile)
        yield

    @pl.when(step == n_blocks // blocks_per_step - 1)
    def _():
        gates = gate_ref[...]
        blk_id = lax.broadcasted_iota(jnp.int32, (n_blocks,) + tile, 0)
        sel = jnp.zeros((n_blocks,) + tile, jnp.bool_)
        for j in range(n_blocks):
            gj = gates[j:j + 1]
            beats = (gates > gj) | ((gates == gj) & (blk_id < j))
            rank = jnp.sum(beats.astype(F32), axis=0, keepdims=True)
            sel = sel | ((blk_id == j) & (rank < MOBA_TOP_K))
        s_own = jnp.broadcast_to(jnp.sum(q * kn_ref[0], axis=-1, keepdims=True), tile)
        ms = jnp.where(sel, m_ref[...], -jnp.inf)
        m_fin = jnp.maximum(jnp.max(ms, axis=0), s_own)
        w = jnp.exp2(ms - m_fin[None])
        w_own = jnp.exp2(s_own - m_fin)
        denom = jnp.sum(w * l_ref[...], axis=0) + w_own
        numer = jnp.sum(w * acc_ref[...], axis=0) + w_own * vn_ref[0]
        o_ref[0] = (numer / denom) * sz_ref[0]


class _Stream(NamedTuple):
    page_table: jax.Array
    q: jax.Array
    k_new: jax.Array
    v_new: jax.Array
    sz: jax.Array
    cache_k: jax.Array
    cache_v: jax.Array
    first_seq: int
    n_seqs: int
    blocks_per_step: int


def _hosted_call(host_kernel, stream, *, name, grid, in_specs, out_specs, out_shape,
                 scratch_shapes, args):
    n_blocks = PAST_LEN // MOBA_BLOCK
    bps = stream.blocks_per_step
    groups = n_blocks // bps
    n_pages = bps * PAGES_PER_BLOCK
    g0, g1 = grid
    assert PAST_LEN % MOBA_BLOCK == 0 and n_blocks % bps == 0 and g0 * g1 == stream.n_seqs * groups
    assert stream.page_table.shape[1] == n_blocks * PAGES_PER_BLOCK
    assert stream.cache_k.shape[0] == 1 and stream.cache_k.shape[2:] == (PAGE_SIZE, N_HEADS, HEAD_DIM)

    n_steps = g0 * g1
    first_seq = stream.first_seq

    def seq_of(u):
        return u // groups

    def step_of(i, j):
        return i * g1 + j

    row_spec = pl.BlockSpec((1, N_HEADS, HEAD_DIM),
                            lambda i, j, pt: (first_seq + seq_of(step_of(i, j)), 0, 0))
    hbm_spec = pl.BlockSpec(memory_space=pl.ANY)
    page_buf = pltpu.VMEM((2, n_pages, PAGE_SIZE, N_HEADS, HEAD_DIM), F32)
    stat = pltpu.VMEM((n_blocks, N_HEADS, HEAD_DIM), F32)
    n_in, n_out, n_scr = len(in_specs), len(out_specs), len(scratch_shapes)

    def body(pt_ref, *refs):
        host_in, refs = refs[:n_in], refs[n_in:]
        rows, (ck_ref, cv_ref), refs = refs[:4], refs[4:6], refs[6:]
        host_out, o_ref, refs = refs[:n_out], refs[n_out], refs[n_out + 1:]
        host_scr, (kbuf, vbuf, sem), stats = refs[:n_scr], refs[n_scr:n_scr + 3], refs[n_scr + 3:]
        u = step_of(pl.program_id(0), pl.program_id(1))
        slot = u % 2

        def page_copies(step, to_slot):
            seq, first_page = first_seq + seq_of(step), (step % groups) * n_pages
            copies = []
            for pg in range(n_pages):
                page = pt_ref[seq, first_page + pg]
                copies.append(pltpu.make_async_copy(
                    ck_ref.at[0, page], kbuf.at[to_slot, pg], sem.at[to_slot, 0]))
                copies.append(pltpu.make_async_copy(
                    cv_ref.at[0, page], vbuf.at[to_slot, pg], sem.at[to_slot, 1]))
            return copies

        def start_all(copies):
            for n, c in enumerate(copies):
                c.start(priority=n % N_DMA_THREADS)

        @pl.when(u == 0)
        def _():
            start_all(page_copies(u, slot))

        @pl.when(u + 1 < n_steps)
        def _():
            start_all(page_copies(u + 1, 1 - slot))

        for c in page_copies(u, slot):
            c.wait()

        host = host_kernel(*host_in, *host_out, *host_scr)
        k_refs = [kbuf.at[slot, pg] for pg in range(n_pages)]
        v_refs = [vbuf.at[slot, pg] for pg in range(n_pages)]
        strm = _stream_body(u % groups, *rows, k_refs, v_refs, o_ref, *stats,
                            n_blocks=n_blocks, blocks_per_step=bps)
        for _ in itertools.zip_longest(host, strm):
            pass

    res = pl.pallas_call(
        body,
        grid_spec=pltpu.PrefetchScalarGridSpec(
            num_scalar_prefetch=1,
            grid=grid,
            in_specs=list(in_specs) + [row_spec] * 4 + [hbm_spec, hbm_spec],
            out_specs=list(out_specs) + [
                pl.BlockSpec((1, N_HEADS, HEAD_DIM),
                             lambda i, j, pt: (seq_of(step_of(i, j)), 0, 0))],
            scratch_shapes=list(scratch_shapes) + [
                page_buf, page_buf, pltpu.SemaphoreType.DMA((2, 2))] + [stat] * 4,
        ),
        out_shape=list(out_shape) + [jax.ShapeDtypeStruct((stream.n_seqs, N_HEADS, HEAD_DIM), F32)],
        compiler_params=pltpu.CompilerParams(
            dimension_semantics=("arbitrary", "arbitrary"),
            vmem_limit_bytes=VMEM_LIMIT),
        name=name,
    )(stream.page_table, *args, stream.q, stream.k_new, stream.v_new, stream.sz,
      stream.cache_k, stream.cache_v)
    return res[:-1], res[-1]


def _out_proj_kernel(a_ref, x_ref, w_ref, g_ref, b_ref, *rest):
    y_ref = rest[-1]
    x = x_ref[...]
    if len(rest) == 3:
        x = _layernorm(x, rest[0][...], rest[1][...])
    y = _dot(a_ref[...].astype(BF16), w_ref[...])
    y_ref[...] = _layernorm(DN_ALPHA * x + y, g_ref[...], b_ref[...])


def _out_proj(a, x, w, g, b, *, tm, norm_in=None):
    rows = x.shape[0]
    assert rows % tm == 0
    row_spec = pl.BlockSpec((tm, D_MODEL), lambda i: (i, 0))
    vec_spec = _const_spec((1, D_MODEL))
    extra = () if norm_in is None else tuple(norm_in)
    return pl.pallas_call(
        _out_proj_kernel,
        grid=(rows // tm,),
        in_specs=[row_spec, row_spec, _const_spec((D_MODEL, D_MODEL)), vec_spec, vec_spec]
        + [vec_spec] * len(extra),
        out_specs=row_spec,
        out_shape=jax.ShapeDtypeStruct((rows, D_MODEL), F32),
        compiler_params=pltpu.CompilerParams(
            dimension_semantics=("arbitrary",), vmem_limit_bytes=VMEM_LIMIT),
        name="out_proj",
    )(a, x, w, g, b, *extra)


def _rope_tables(pos):
    inv = ROPE_THETA ** (-np.arange(ROT_HALF, dtype=np.float64) * 2.0 / ROT_DIM)
    ang = np.asarray(pos, np.float64)[:, None] * inv[None, :]
    cos, sin = np.cos(ang), np.sin(ang)
    n = ang.shape[0]
    rest = HEAD_DIM - ROT_DIM
    cos_t = np.concatenate([cos, cos, np.ones((n, rest))], axis=1)
    sin_up = np.concatenate([np.zeros((n, ROT_HALF)), sin, np.zeros((n, rest))], axis=1)
    sin_dn = np.concatenate([-sin, np.zeros((n, ROT_HALF + rest))], axis=1)
    return tuple(jnp.asarray(t, F32) for t in (cos_t, sin_up, sin_dn))


def kernel(x_prompt, x_sample, state_conv, cache_k, cache_v, page_table,
           w_in_conv, w_conv, w_out_conv, w_in_attn, w_out_attn, ln_g, ln_b):
    bsz, seq, _ = x_prompt.shape
    rows, dec_seq, _ = x_sample.shape
    assert dec_seq == 1 and w_in_conv.shape[0] == 1 and w_in_attn.shape[0] == 1
    win_c, wout_c = w_in_conv[0].astype(BF16), w_out_conv[0].astype(BF16)
    win_a, wout_a = w_in_attn[0].astype(BF16), w_out_attn[0].astype(BF16)
    g0, b0, g1, b1 = ln_g[0:1], ln_b[0:1], ln_g[1:2], ln_b[1:2]

    assert CONV_W == 3
    xs1, conv_s = _conv_sample(x_sample, state_conv[0], win_c, w_conv[0], wout_c, g0, b0)
    conv_s = conv_s[None]
    qs, ks, vs, szs = _attn_proj_sample(xs1, win_a, *_rope_tables(np.full((rows,), PAST_LEN)))
    per_row = lambda t: t.reshape(rows, N_HEADS, HEAD_DIM)
    n_blocks = PAST_LEN // MOBA_BLOCK
    quarter = rows // 4

    def stream(first_seq, n_seqs, blocks_per_step):
        return _Stream(page_table, per_row(qs), per_row(ks), per_row(vs), per_row(szs),
                       cache_k, cache_v, first_seq, n_seqs, blocks_per_step)

    (rp0, conv_tail), o_a = _conv_prompt(x_prompt, win_c, w_conv[0], wout_c,
                                         stream(0, quarter, n_blocks // 2))
    conv_p = conv_tail[:, :, SUBLANES - (CONV_W - 1):, :]
    (q, k_p, v_p, kb, vb, sz, kmean), o_b = _attn_proj_prompt(
        rp0, g0, b0, win_a, *_rope_tables(np.arange(seq)), stream(quarter, quarter, n_blocks // 2))
    (o,), o_c = _moba_prompt(q, kb, vb, kmean, sz, stream(2 * quarter, 2 * quarter, n_blocks))
    y_p = _out_proj(o.reshape(bsz * seq, D_MODEL), rp0.reshape(bsz * seq, D_MODEL),
                    wout_a, g1, b1, tm=OUT_ROW_TILE, norm_in=(g0, b0))
    y_p = y_p.reshape(bsz, seq, D_MODEL)

    os_ = jnp.concatenate([o_a, o_b, o_c], axis=0)
    y_s = _out_proj(os_.reshape(rows, D_MODEL), xs1, wout_a, g1, b1, tm=rows)
    y_s = y_s.reshape(rows, 1, D_MODEL)
    new_shape = (1, rows, 1, N_HEADS, HEAD_DIM)
    return (y_p, y_s, conv_p, conv_s, k_p[None], v_p[None],
            ks.reshape(new_shape), vs.reshape(new_shape))
```

```python
import functools
import itertools
from typing import NamedTuple

import numpy as np
import jax
import jax.numpy as jnp
from jax import lax
from jax.experimental import pallas as pl
from jax.experimental.pallas import tpu as pltpu

D_MODEL = 1024
DEPTH = 2
PAST_LEN = 2048
PAGE_SIZE = 128
N_HEADS = 8
HEAD_DIM = D_MODEL // N_HEADS
ROT_DIM = HEAD_DIM // 4
ROT_HALF = ROT_DIM // 2
ROPE_THETA = 500000.0
MOBA_BLOCK = 256
MOBA_TOP_K = 3
CONV_W = 3
DN_ALPHA = (2.0 * DEPTH) ** 0.25
LN_EPS = 1e-5
ATTN_SCALE = HEAD_DIM ** -0.5
LOG2E = 1.4426950408889634
PAGES_PER_BLOCK = MOBA_BLOCK // PAGE_SIZE

SUBLANES = 8
BF16_SUBLANES = 16
ROW_TILE = 256
OUT_ROW_TILE = 1024
COL_CHUNK = 256
STREAM_CHUNK = 16
STREAM_SUM_PARTS = 2
BULK_DMA_THREAD = 1
VMEM_LIMIT = 56 * 1024 * 1024

BF16 = jnp.bfloat16
F32 = jnp.float32


def _dot(a, b):
    return jnp.dot(a, b, preferred_element_type=F32)


def _dot_nt(a, b):
    return lax.dot_general(a, b, (((1,), (1,)), ((), ())), preferred_element_type=F32)


def _layernorm(r, g, b):
    mu = jnp.mean(r, axis=-1, keepdims=True)
    c = r - mu
    var = jnp.mean(c * c, axis=-1, keepdims=True)
    return c * lax.rsqrt(var + LN_EPS) * g + b


def _silu(z):
    return z * jax.nn.sigmoid(z)


def _const_spec(shape):
    return pl.BlockSpec(shape, lambda *_: (0,) * len(shape), pipeline_mode=pl.Buffered(1))


def _conv_prompt_kernel(x_ref, win_ref, wconv_ref, wout_ref,
                        r_ref, state_ref, ext_ref, gate_ref, *, tm):
    t = pl.program_id(1)

    @pl.when(t == 0)
    def _():
        ext_ref[0:SUBLANES, :] = jnp.zeros((SUBLANES, D_MODEL), F32)

    x = x_ref[0]
    xb = x.astype(BF16)
    for c in range(D_MODEL // COL_CHUNK):
        lo = c * COL_CHUNK
        cols = slice(lo, lo + COL_CHUNK)
        b_gate = _dot(xb, win_ref[:, 0 * D_MODEL + lo:0 * D_MODEL + lo + COL_CHUNK])
        c_gate = _dot(xb, win_ref[:, 1 * D_MODEL + lo:1 * D_MODEL + lo + COL_CHUNK])
        h = _dot(xb, win_ref[:, 2 * D_MODEL + lo:2 * D_MODEL + lo + COL_CHUNK])
        z = _dot(xb, win_ref[:, 3 * D_MODEL + lo:3 * D_MODEL + lo + COL_CHUNK])
        u = c_gate * h
        ext_ref[SUBLANES:SUBLANES + tm, cols] = u
        ext = ext_ref[:, cols]
        u1 = pltpu.roll(ext, 1, 0)[SUBLANES:]
        u2 = pltpu.roll(ext, 2, 0)[SUBLANES:]
        conv = (wconv_ref[0:1, cols] * u2 + wconv_ref[1:2, cols] * u1
                + wconv_ref[2:3, cols] * u)
        gate_ref[:, cols] = (b_gate * conv * _silu(z)).astype(BF16)
        yield
    tail = ext_ref[tm:tm + SUBLANES, :]
    ext_ref[0:SUBLANES, :] = tail
    state_ref[0, 0] = tail
    r_ref[0] = DN_ALPHA * x + _dot(gate_ref[...], wout_ref[...])


def _conv_prompt(x, w_in, w_conv, w_out, stream):
    bsz, seq, _ = x.shape
    tm = ROW_TILE
    assert seq % tm == 0
    return _hosted_call(
        functools.partial(_conv_prompt_kernel, tm=tm), stream,
        name="conv_prompt",
        grid=(bsz, seq // tm),
        in_specs=[
            pl.BlockSpec((1, tm, D_MODEL), lambda i, j, *_: (i, j, 0)),
            _const_spec((D_MODEL, 4 * D_MODEL)),
            _const_spec((CONV_W, D_MODEL)),
            _const_spec((D_MODEL, D_MODEL)),
        ],
        out_specs=[
            pl.BlockSpec((1, tm, D_MODEL), lambda i, j, *_: (i, j, 0)),
            pl.BlockSpec((1, 1, SUBLANES, D_MODEL), lambda i, j, *_: (0, i, 0, 0)),
        ],
        out_shape=[
            jax.ShapeDtypeStruct((bsz, seq, D_MODEL), F32),
            jax.ShapeDtypeStruct((1, bsz, SUBLANES, D_MODEL), F32),
        ],
        scratch_shapes=[
            pltpu.VMEM((tm + SUBLANES, D_MODEL), F32),
            pltpu.VMEM((tm, D_MODEL), BF16),
        ],
        args=(x, w_in, w_conv, w_out),
    )


def _conv_sample_kernel(x_ref, prev_ref, win_ref, wconv_ref, wout_ref, g_ref, b_ref,
                        y_ref, state_ref):
    x = x_ref[:, 0, :]
    xb = x.astype(BF16)
    b_gate = _dot(xb, win_ref[:, 0 * D_MODEL:1 * D_MODEL])
    c_gate = _dot(xb, win_ref[:, 1 * D_MODEL:2 * D_MODEL])
    h = _dot(xb, win_ref[:, 2 * D_MODEL:3 * D_MODEL])
    z = _dot(xb, win_ref[:, 3 * D_MODEL:4 * D_MODEL])
    u = c_gate * h
    prev0 = prev_ref[:, 0, :]
    prev1 = prev_ref[:, 1, :]
    conv = wconv_ref[0:1, :] * prev0 + wconv_ref[1:2, :] * prev1 + wconv_ref[2:3, :] * u
    gate = (b_gate * conv * _silu(z)).astype(BF16)
    y = _dot(gate, wout_ref[...])
    y_ref[...] = _layernorm(DN_ALPHA * x + y, g_ref[...], b_ref[...])
    state_ref[:, 0, :] = prev1
    state_ref[:, 1, :] = u


def _conv_sample(x, prev, w_in, w_conv, w_out, g, b):
    rows = x.shape[0]
    return pl.pallas_call(
        _conv_sample_kernel,
        out_shape=[
            jax.ShapeDtypeStruct((rows, D_MODEL), F32),
            jax.ShapeDtypeStruct((rows, CONV_W - 1, D_MODEL), F32),
        ],
        compiler_params=pltpu.CompilerParams(vmem_limit_bytes=VMEM_LIMIT),
        name="conv_sample",
    )(x, prev, w_in, w_conv, w_out, g, b)


def _rope(x, cos_ref, sin_up_ref, sin_dn_ref):
    cos, sin_up, sin_dn = cos_ref[...], sin_up_ref[...], sin_dn_ref[...]
    heads = []
    for hd in range(N_HEADS):
        xh = x[:, hd * HEAD_DIM:(hd + 1) * HEAD_DIM]
        heads.append(xh * cos
                     + pltpu.roll(xh, ROT_HALF, 1) * sin_up
                     + pltpu.roll(xh, HEAD_DIM - ROT_HALF, 1) * sin_dn)
    return jnp.concatenate(heads, axis=1)


def _store_paged(ref, val):
    groups = PAGE_SIZE // SUBLANES
    sub = lax.broadcasted_iota(jnp.int32, (groups, SUBLANES, HEAD_DIM), 1)
    for pg in range(ref.shape[0]):
        slab = val[pg * PAGE_SIZE:(pg + 1) * PAGE_SIZE]
        parts = [slab[:, hd * HEAD_DIM:(hd + 1) * HEAD_DIM].reshape(groups, SUBLANES, HEAD_DIM)
                 for hd in range(N_HEADS)]
        dist = N_HEADS // 2
        while dist >= 1:
            keep = (sub & dist) == 0
            nxt = list(parts)
            for a in range(N_HEADS):
                if a & dist == 0:
                    b = a + dist
                    nxt[a] = jnp.where(keep, parts[a], pltpu.roll(parts[b], dist, 1))
                    nxt[b] = jnp.where(keep, pltpu.roll(parts[a], SUBLANES - dist, 1), parts[b])
            parts = nxt
            dist //= 2
        page = jnp.stack(parts, axis=1).reshape(PAGE_SIZE, N_HEADS, HEAD_DIM)
        ref[pg] = page.astype(ref.dtype)


def _attn_proj_prompt_kernel(r_ref, g_ref, b_ref, win_ref, cos_ref, sin_up_ref, sin_dn_ref,
                             q_ref, k_ref, v_ref, kb_ref, vb_ref, sz_ref, kmean_ref, *, tm):
    xb = _layernorm(r_ref[0], g_ref[...], b_ref[...]).astype(BF16)
    q = _rope(_dot(xb, win_ref[:, 0 * D_MODEL:1 * D_MODEL]), cos_ref, sin_up_ref, sin_dn_ref)
    q_ref[0] = (q * (ATTN_SCALE * LOG2E)).astype(BF16)
    yield
    k = _rope(_dot(xb, win_ref[:, 1 * D_MODEL:2 * D_MODEL]), cos_ref, sin_up_ref, sin_dn_ref)
    _store_paged(k_ref.at[0], k)
    kb_ref[0] = k.astype(BF16)
    t = pl.program_id(1)
    for j in range(tm // MOBA_BLOCK):
        blk = k[j * MOBA_BLOCK:(j + 1) * MOBA_BLOCK]
        kmean_ref[0, pl.ds(t * (tm // MOBA_BLOCK) + j, 1), :] = (
            jnp.sum(blk, axis=0, keepdims=True) * (1.0 / MOBA_BLOCK))
    yield
    v = _dot(xb, win_ref[:, 2 * D_MODEL:3 * D_MODEL])
    _store_paged(v_ref.at[0], v)
    vb_ref[0] = v.astype(BF16)
    yield
    z = _dot(xb, win_ref[:, 3 * D_MODEL:4 * D_MODEL])
    sz_ref[0] = _silu(z).astype(BF16)


def _attn_proj_prompt(r, g, b, w_in, cos, sin_up, sin_dn, stream):
    bsz, seq, _ = r.shape
    tm = ROW_TILE
    assert seq % tm == 0 and tm % MOBA_BLOCK == 0 and seq // MOBA_BLOCK == SUBLANES
    row_spec = pl.BlockSpec((1, tm, D_MODEL), lambda i, j, *_: (i, j, 0))
    tab_spec = pl.BlockSpec((tm, HEAD_DIM), lambda i, j, *_: (j, 0))
    paged_spec = pl.BlockSpec((1, tm // PAGE_SIZE, PAGE_SIZE, N_HEADS, HEAD_DIM),
                              lambda i, j, *_: (i, j, 0, 0, 0))
    flat = lambda dt: jax.ShapeDtypeStruct((bsz, seq, D_MODEL), dt)
    paged = jax.ShapeDtypeStruct((bsz, seq // PAGE_SIZE, PAGE_SIZE, N_HEADS, HEAD_DIM), F32)
    return _hosted_call(
        functools.partial(_attn_proj_prompt_kernel, tm=tm), stream,
        name="attn_proj_prompt",
        grid=(bsz, seq // tm),
        in_specs=[row_spec, _const_spec((1, D_MODEL)), _const_spec((1, D_MODEL)),
                  _const_spec((D_MODEL, 4 * D_MODEL)), tab_spec, tab_spec, tab_spec],
        out_specs=[row_spec, paged_spec, paged_spec, row_spec, row_spec, row_spec,
                   pl.BlockSpec((1, SUBLANES, D_MODEL), lambda i, j, *_: (i, 0, 0))],
        out_shape=[flat(BF16), paged, paged, flat(BF16), flat(BF16), flat(BF16),
                   jax.ShapeDtypeStruct((bsz, SUBLANES, D_MODEL), F32)],
        scratch_shapes=[],
        args=(r, g, b, w_in, cos, sin_up, sin_dn),
    )


def _attn_proj_sample_kernel(x_ref, win_ref, cos_ref, sin_up_ref, sin_dn_ref,
                             q_ref, k_ref, v_ref, sz_ref):
    xb = x_ref[...].astype(BF16)
    q = _rope(_dot(xb, win_ref[:, 0 * D_MODEL:1 * D_MODEL]), cos_ref, sin_up_ref, sin_dn_ref)
    _store_paged(q_ref, q * (ATTN_SCALE * LOG2E))
    k = _rope(_dot(xb, win_ref[:, 1 * D_MODEL:2 * D_MODEL]), cos_ref, sin_up_ref, sin_dn_ref)
    _store_paged(k_ref, k)
    _store_paged(v_ref, _dot(xb, win_ref[:, 2 * D_MODEL:3 * D_MODEL]))
    _store_paged(sz_ref, _silu(_dot(xb, win_ref[:, 3 * D_MODEL:4 * D_MODEL])))


def _attn_proj_sample(x, w_in, cos, sin_up, sin_dn):
    rows = x.shape[0]
    assert rows % PAGE_SIZE == 0
    paged = jax.ShapeDtypeStruct((rows // PAGE_SIZE, PAGE_SIZE, N_HEADS, HEAD_DIM), F32)
    return pl.pallas_call(
        _attn_proj_sample_kernel,
        out_shape=[paged, paged, paged, paged],
        compiler_params=pltpu.CompilerParams(vmem_limit_bytes=VMEM_LIMIT),
        name="attn_proj_sample",
    )(x, w_in, cos, sin_up, sin_dn)


def _moba_prompt_kernel(q_ref, k_ref, v_ref, kmean_ref, sz_ref, o_ref, *, n_blocks):
    blk = MOBA_BLOCK
    kmean = kmean_ref[0].astype(BF16)
    v_t = jnp.concatenate([v_ref[0].T,
                           jnp.ones((BF16_SUBLANES, v_ref.shape[1]), BF16)], axis=0)
    blk_id = lax.broadcasted_iota(jnp.int32, (n_blocks, blk), 0)
    key = lax.broadcasted_iota(jnp.int32, (blk, blk), 0)
    qry = lax.broadcasted_iota(jnp.int32, (blk, blk), 1)
    causal = key <= qry
    for i in range(n_blocks):
        qi = q_ref[0, i * blk:(i + 1) * blk, :]
        scores = []
        if i > MOBA_TOP_K:
            gate = _dot_nt(kmean, qi)
        for n in range(i):
            s = _dot_nt(k_ref[0, n * blk:(n + 1) * blk, :], qi)
            if i > MOBA_TOP_K:
                gn = gate[n:n + 1, :]
                beats = ((gate > gn) | ((gate == gn) & (blk_id < n))) & (blk_id < i)
                rank = jnp.sum(beats.astype(F32), axis=0, keepdims=True)
                s = jnp.where(rank < MOBA_TOP_K, s, -jnp.inf)
            scores.append(s)
        s_own = _dot_nt(k_ref[0, i * blk:(i + 1) * blk, :], qi)
        scores.append(jnp.where(causal, s_own, -jnp.inf))
        m = scores[0].max(axis=0, keepdims=True)
        for s in scores[1:]:
            m = jnp.maximum(m, s.max(axis=0, keepdims=True))
        acc = jnp.zeros((HEAD_DIM + BF16_SUBLANES, blk), F32)
        for n, s in enumerate(scores):
            p = jnp.exp2(s - m).astype(BF16)
            acc = acc + _dot(v_t[:, n * blk:(n + 1) * blk], p)
        o = (acc[:HEAD_DIM] / acc[HEAD_DIM:HEAD_DIM + 1]).T
        o_ref[0, i * blk:(i + 1) * blk, :] = (
            o * sz_ref[0, i * blk:(i + 1) * blk, :].astype(F32)).astype(BF16)
        yield


def _moba_prompt(q, k, v, kmean, sz, stream):
    bsz, seq, _ = q.shape
    n_blocks = seq // MOBA_BLOCK
    assert n_blocks == SUBLANES
    head_spec = pl.BlockSpec((1, seq, HEAD_DIM), lambda i, j, *_: (i, 0, j))
    return _hosted_call(
        functools.partial(_moba_prompt_kernel, n_blocks=n_blocks), stream,
        name="moba_prompt",
        grid=(bsz, N_HEADS),
        in_specs=[head_spec, head_spec, head_spec,
                  pl.BlockSpec((1, n_blocks, HEAD_DIM), lambda i, j, *_: (i, 0, j)),
                  head_spec],
        out_specs=[head_spec],
        out_shape=[jax.ShapeDtypeStruct((bsz, seq, D_MODEL), BF16)],
        scratch_shapes=[],
        args=(q, k, v, kmean, sz),
    )


def _sum_tokens(x):
    t = x.shape[0]
    parts = jnp.sum(x.reshape((STREAM_SUM_PARTS, t // STREAM_SUM_PARTS) + x.shape[1:]), axis=1)
    return jnp.sum(parts, axis=0)


def _stream_body(step, q_ref, kn_ref, vn_ref, sz_ref, k_refs, v_refs,
                 o_ref, m_ref, l_ref, acc_ref, gate_ref, *, n_blocks, blocks_per_step):
    tile = (N_HEADS, HEAD_DIM)
    q = q_ref[0]
    for j in range(blocks_per_step):
        m = jnp.full((N_HEADS, 1), -jnp.inf, F32)
        l = jnp.zeros((N_HEADS, 1), F32)
        acc = jnp.zeros(tile, F32)
        ksum = jnp.zeros(tile, F32)
        for pg in range(j * PAGES_PER_BLOCK, (j + 1) * PAGES_PER_BLOCK):
            for c in range(PAGE_SIZE // STREAM_CHUNK):
                tok = slice(c * STREAM_CHUNK, (c + 1) * STREAM_CHUNK)
                kc = k_refs[pg][tok]
                s = jnp.sum(kc * q[None], axis=-1, keepdims=True)
                m_new = jnp.maximum(m, jnp.max(s, axis=0))
                alpha = jnp.exp2(m - m_new)
                p = jnp.exp2(s - m_new[None])
                l = l * alpha + _sum_tokens(p)
                acc = acc * alpha + _sum_tokens(p * v_refs[pg][tok])
                ksum = ksum + _sum_tokens(kc)
                m = m_new
        kmean = ksum * (1.0 / MOBA_BLOCK)
        n = step * blocks_per_step + j
        m_ref[n] = jnp.broadcast_to(m, tile)
        l_ref[n] = jnp.broadcast_to(l, tile)
        acc_ref[n] = acc
        gate_ref[n] = jnp.broadcast_to(jnp.sum(q * kmean, axis=-1, keepdims=True), tile)
        yield

    @pl.when(step == n_blocks // blocks_per_step - 1)
    def _():
        gates = gate_ref[...]
        blk_id = lax.broadcasted_iota(jnp.int32, (n_blocks,) + tile, 0)
        sel = jnp.zeros((n_blocks,) + tile, jnp.bool_)
        for j in range(n_blocks):
            gj = gates[j:j + 1]
            beats = (gates > gj) | ((gates == gj) & (blk_id < j))
            rank = jnp.sum(beats.astype(F32), axis=0, keepdims=True)
            sel = sel | ((blk_id == j) & (rank < MOBA_TOP_K))
        s_own = jnp.broadcast_to(jnp.sum(q * kn_ref[0], axis=-1, keepdims=True), tile)
        ms = jnp.where(sel, m_ref[...], -jnp.inf)
        m_fin = jnp.maximum(jnp.max(ms, axis=0), s_own)
        w = jnp.exp2(ms - m_fin[None])
        w_own = jnp.exp2(s_own - m_fin)
        denom = jnp.sum(w * l_ref[...], axis=0) + w_own
        numer = jnp.sum(w * acc_ref[...], axis=0) + w_own * vn_ref[0]
        o_ref[0] = (numer / denom) * sz_ref[0]


class _Stream(NamedTuple):
    page_table: jax.Array
    q: jax.Array
    k_new: jax.Array
    v_new: jax.Array
    sz: jax.Array
    cache_k: jax.Array
    cache_v: jax.Array
    first_seq: int
    n_seqs: int
    blocks_per_step: int


def _hosted_call(host_kernel, stream, *, name, grid, in_specs, out_specs, out_shape,
                 scratch_shapes, args):
    n_blocks = PAST_LEN // MOBA_BLOCK
    bps = stream.blocks_per_step
    groups = n_blocks // bps
    n_pages = bps * PAGES_PER_BLOCK
    g0, g1 = grid
    assert PAST_LEN % MOBA_BLOCK == 0 and n_blocks % bps == 0 and g0 * g1 == stream.n_seqs * groups
    assert stream.page_table.shape[1] == n_blocks * PAGES_PER_BLOCK
    assert stream.cache_k.shape[0] == 1 and stream.cache_k.shape[2:] == (PAGE_SIZE, N_HEADS, HEAD_DIM)

    n_steps = g0 * g1
    first_seq = stream.first_seq

    def seq_of(u):
        return u // groups

    def step_of(i, j):
        return i * g1 + j

    row_spec = pl.BlockSpec((1, N_HEADS, HEAD_DIM),
                            lambda i, j, pt: (first_seq + seq_of(step_of(i, j)), 0, 0))
    hbm_spec = pl.BlockSpec(memory_space=pl.ANY)
    page_buf = pltpu.VMEM((2, n_pages, PAGE_SIZE, N_HEADS, HEAD_DIM), F32)
    stat = pltpu.VMEM((n_blocks, N_HEADS, HEAD_DIM), F32)
    n_in, n_out, n_scr = len(in_specs), len(out_specs), len(scratch_shapes)

    def body(pt_ref, *refs):
        host_in, refs = refs[:n_in], refs[n_in:]
        rows, (ck_ref, cv_ref), refs = refs[:4], refs[4:6], refs[6:]
        host_out, o_ref, refs = refs[:n_out], refs[n_out], refs[n_out + 1:]
        host_scr, (kbuf, vbuf, sem), stats = refs[:n_scr], refs[n_scr:n_scr + 3], refs[n_scr + 3:]
        u = step_of(pl.program_id(0), pl.program_id(1))
        slot = u % 2

        def page_copies(step, to_slot):
            seq, first_page = first_seq + seq_of(step), (step % groups) * n_pages
            copies = []
            for pg in range(n_pages):
                page = pt_ref[seq, first_page + pg]
                copies.append(pltpu.make_async_copy(
                    ck_ref.at[0, page], kbuf.at[to_slot, pg], sem.at[to_slot, 0]))
                copies.append(pltpu.make_async_copy(
                    cv_ref.at[0, page], vbuf.at[to_slot, pg], sem.at[to_slot, 1]))
            return copies

        def start_all(copies):
            for c in copies:
                c.start(priority=BULK_DMA_THREAD)

        @pl.when(u == 0)
        def _():
            start_all(page_copies(u, slot))

        @pl.when(u + 1 < n_steps)
        def _():
            start_all(page_copies(u + 1, 1 - slot))

        for c in page_copies(u, slot):
            c.wait()

        host = host_kernel(*host_in, *host_out, *host_scr)
        k_refs = [kbuf.at[slot, pg] for pg in range(n_pages)]
        v_refs = [vbuf.at[slot, pg] for pg in range(n_pages)]
        strm = _stream_body(u % groups, *rows, k_refs, v_refs, o_ref, *stats,
                            n_blocks=n_blocks, blocks_per_step=bps)
        for _ in itertools.zip_longest(host, strm):
            pass

    res = pl.pallas_call(
        body,
        grid_spec=pltpu.PrefetchScalarGridSpec(
            num_scalar_prefetch=1,
            grid=grid,
            in_specs=list(in_specs) + [row_spec] * 4 + [hbm_spec, hbm_spec],
            out_specs=list(out_specs) + [
                pl.BlockSpec((1, N_HEADS, HEAD_DIM),
                             lambda i, j, pt: (seq_of(step_of(i, j)), 0, 0))],
            scratch_shapes=list(scratch_shapes) + [
                page_buf, page_buf, pltpu.SemaphoreType.DMA((2, 2))] + [stat] * 4,
        ),
        out_shape=list(out_shape) + [jax.ShapeDtypeStruct((stream.n_seqs, N_HEADS, HEAD_DIM), F32)],
        compiler_params=pltpu.CompilerParams(
            dimension_semantics=("arbitrary", "arbitrary"),
            vmem_limit_bytes=VMEM_LIMIT),
        name=name,
    )(stream.page_table, *args, stream.q, stream.k_new, stream.v_new, stream.sz,
      stream.cache_k, stream.cache_v)
    return res[:-1], res[-1]


def _out_proj_kernel(a_ref, x_ref, w_ref, g_ref, b_ref, *rest):
    y_ref = rest[-1]
    x = x_ref[...]
    if len(rest) == 3:
        x = _layernorm(x, rest[0][...], rest[1][...])
    y = _dot(a_ref[...].astype(BF16), w_ref[...])
    y_ref[...] = _layernorm(DN_ALPHA * x + y, g_ref[...], b_ref[...])


def _out_proj(a, x, w, g, b, *, tm, norm_in=None):
    rows = x.shape[0]
    assert rows % tm == 0
    row_spec = pl.BlockSpec((tm, D_MODEL), lambda i: (i, 0))
    vec_spec = _const_spec((1, D_MODEL))
    extra = () if norm_in is None else tuple(norm_in)
    return pl.pallas_call(
        _out_proj_kernel,
        grid=(rows // tm,),
        in_specs=[row_spec, row_spec, _const_spec((D_MODEL, D_MODEL)), vec_spec, vec_spec]
        + [vec_spec] * len(extra),
        out_specs=row_spec,
        out_shape=jax.ShapeDtypeStruct((rows, D_MODEL), F32),
        compiler_params=pltpu.CompilerParams(
            dimension_semantics=("arbitrary",), vmem_limit_bytes=VMEM_LIMIT),
        name="out_proj",
    )(a, x, w, g, b, *extra)


def _rope_tables(pos):
    inv = ROPE_THETA ** (-np.arange(ROT_HALF, dtype=np.float64) * 2.0 / ROT_DIM)
    ang = np.asarray(pos, np.float64)[:, None] * inv[None, :]
    cos, sin = np.cos(ang), np.sin(ang)
    n = ang.shape[0]
    rest = HEAD_DIM - ROT_DIM
    cos_t = np.concatenate([cos, cos, np.ones((n, rest))], axis=1)
    sin_up = np.concatenate([np.zeros((n, ROT_HALF)), sin, np.zeros((n, rest))], axis=1)
    sin_dn = np.concatenate([-sin, np.zeros((n, ROT_HALF + rest))], axis=1)
    return tuple(jnp.asarray(t, F32) for t in (cos_t, sin_up, sin_dn))


def kernel(x_prompt, x_sample, state_conv, cache_k, cache_v, page_table,
           w_in_conv, w_conv, w_out_conv, w_in_attn, w_out_attn, ln_g, ln_b):
    bsz, seq, _ = x_prompt.shape
    rows, dec_seq, _ = x_sample.shape
    assert dec_seq == 1 and w_in_conv.shape[0] == 1 and w_in_attn.shape[0] == 1
    win_c, wout_c = w_in_conv[0].astype(BF16), w_out_conv[0].astype(BF16)
    win_a, wout_a = w_in_attn[0].astype(BF16), w_out_attn[0].astype(BF16)
    g0, b0, g1, b1 = ln_g[0:1], ln_b[0:1], ln_g[1:2], ln_b[1:2]

    assert CONV_W == 3
    xs1, conv_s = _conv_sample(x_sample, state_conv[0], win_c, w_conv[0], wout_c, g0, b0)
    conv_s = conv_s[None]
    qs, ks, vs, szs = _attn_proj_sample(xs1, win_a, *_rope_tables(np.full((rows,), PAST_LEN)))
    per_row = lambda t: t.reshape(rows, N_HEADS, HEAD_DIM)
    n_blocks = PAST_LEN // MOBA_BLOCK
    quarter = rows // 4

    def stream(first_seq, n_seqs, blocks_per_step):
        return _Stream(page_table, per_row(qs), per_row(ks), per_row(vs), per_row(szs),
                       cache_k, cache_v, first_seq, n_seqs, blocks_per_step)

    (rp0, conv_tail), o_a = _conv_prompt(x_prompt, win_c, w_conv[0], wout_c,
                                         stream(0, quarter, n_blocks // 2))
    conv_p = conv_tail[:, :, SUBLANES - (CONV_W - 1):, :]
    (q, k_p, v_p, kb, vb, sz, kmean), o_b = _attn_proj_prompt(
        rp0, g0, b0, win_a, *_rope_tables(np.arange(seq)), stream(quarter, quarter, n_blocks // 2))
    (o,), o_c = _moba_prompt(q, kb, vb, kmean, sz, stream(2 * quarter, 2 * quarter, n_blocks))
    y_p = _out_proj(o.reshape(bsz * seq, D_MODEL), rp0.reshape(bsz * seq, D_MODEL),
                    wout_a, g1, b1, tm=OUT_ROW_TILE, norm_in=(g0, b0))
    y_p = y_p.reshape(bsz, seq, D_MODEL)

    os_ = jnp.concatenate([o_a, o_b, o_c], axis=0)
    y_s = _out_proj(os_.reshape(rows, D_MODEL), xs1, wout_a, g1, b1, tm=rows)
    y_s = y_s.reshape(rows, 1, D_MODEL)
    new_shape = (1, rows, 1, N_HEADS, HEAD_DIM)
    return (y_p, y_s, conv_p, conv_s, k_p[None], v_p[None],
            ks.reshape(new_shape), vs.reshape(new_shape))
```

```python
import functools
import itertools
from typing import NamedTuple

import numpy as np
import jax
import jax.numpy as jnp
from jax import lax
from jax.experimental import pallas as pl
from jax.experimental.pallas import tpu as pltpu

D_MODEL = 1024
DEPTH = 2
PAST_LEN = 2048
PAGE_SIZE = 128
N_HEADS = 8
HEAD_DIM = D_MODEL // N_HEADS
ROT_DIM = HEAD_DIM // 4
ROT_HALF = ROT_DIM // 2
ROPE_THETA = 500000.0
MOBA_BLOCK = 256
MOBA_TOP_K = 3
CONV_W = 3
DN_ALPHA = (2.0 * DEPTH) ** 0.25
LN_EPS = 1e-5
ATTN_SCALE = HEAD_DIM ** -0.5
LOG2E = 1.4426950408889634
PAGES_PER_BLOCK = MOBA_BLOCK // PAGE_SIZE

SUBLANES = 8
BF16_SUBLANES = 16
ROW_TILE = 256
OUT_ROW_TILE = 1024
COL_CHUNK = 256
STREAM_CHUNK = 16
STREAM_SUM_PARTS = 2
BULK_DMA_THREAD = 1
VMEM_LIMIT = 56 * 1024 * 1024

BF16 = jnp.bfloat16
F32 = jnp.float32


def _dot(a, b):
    return jnp.dot(a, b, preferred_element_type=F32)


def _dot_nt(a, b):
    return lax.dot_general(a, b, (((1,), (1,)), ((), ())), preferred_element_type=F32)


def _layernorm(r, g, b):
    mu = jnp.mean(r, axis=-1, keepdims=True)
    c = r - mu
    var = jnp.mean(c * c, axis=-1, keepdims=True)
    return c * lax.rsqrt(var + LN_EPS) * g + b


def _silu(z):
    return z * jax.nn.sigmoid(z)


def _const_spec(shape):
    return pl.BlockSpec(shape, lambda *_: (0,) * len(shape), pipeline_mode=pl.Buffered(1))


def _conv_prompt_kernel(x_ref, win_ref, wconv_ref, wout_ref,
                        r_ref, state_ref, ext_ref, gate_ref, *, tm):
    t = pl.program_id(1)

    @pl.when(t == 0)
    def _():
        ext_ref[0:SUBLANES, :] = jnp.zeros((SUBLANES, D_MODEL), F32)

    x = x_ref[0]
    xb = x.astype(BF16)
    for c in range(D_MODEL // COL_CHUNK):
        lo = c * COL_CHUNK
        cols = slice(lo, lo + COL_CHUNK)
        b_gate = _dot(xb, win_ref[:, 0 * D_MODEL + lo:0 * D_MODEL + lo + COL_CHUNK])
        c_gate = _dot(xb, win_ref[:, 1 * D_MODEL + lo:1 * D_MODEL + lo + COL_CHUNK])
        h = _dot(xb, win_ref[:, 2 * D_MODEL + lo:2 * D_MODEL + lo + COL_CHUNK])
        z = _dot(xb, win_ref[:, 3 * D_MODEL + lo:3 * D_MODEL + lo + COL_CHUNK])
        u = c_gate * h
        ext_ref[SUBLANES:SUBLANES + tm, cols] = u
        ext = ext_ref[:, cols]
        u1 = pltpu.roll(ext, 1, 0)[SUBLANES:]
        u2 = pltpu.roll(ext, 2, 0)[SUBLANES:]
        conv = (wconv_ref[0:1, cols] * u2 + wconv_ref[1:2, cols] * u1
                + wconv_ref[2:3, cols] * u)
        gate_ref[:, cols] = (b_gate * conv * _silu(z)).astype(BF16)
        yield
    tail = ext_ref[tm:tm + SUBLANES, :]
    ext_ref[0:SUBLANES, :] = tail
    state_ref[0, 0] = tail
    r_ref[0] = DN_ALPHA * x + _dot(gate_ref[...], wout_ref[...])


def _conv_prompt(x, w_in, w_conv, w_out, stream):
    bsz, seq, _ = x.shape
    tm = ROW_TILE
    assert seq % tm == 0
    return _hosted_call(
        functools.partial(_conv_prompt_kernel, tm=tm), stream,
        name="conv_prompt",
        grid=(bsz, seq // tm),
        in_specs=[
            pl.BlockSpec((1, tm, D_MODEL), lambda i, j, *_: (i, j, 0)),
            _const_spec((D_MODEL, 4 * D_MODEL)),
            _const_spec((CONV_W, D_MODEL)),
            _const_spec((D_MODEL, D_MODEL)),
        ],
        out_specs=[
            pl.BlockSpec((1, tm, D_MODEL), lambda i, j, *_: (i, j, 0)),
            pl.BlockSpec((1, 1, SUBLANES, D_MODEL), lambda i, j, *_: (0, i, 0, 0)),
        ],
        out_shape=[
            jax.ShapeDtypeStruct((bsz, seq, D_MODEL), F32),
            jax.ShapeDtypeStruct((1, bsz, SUBLANES, D_MODEL), F32),
        ],
        scratch_shapes=[
            pltpu.VMEM((tm + SUBLANES, D_MODEL), F32),
            pltpu.VMEM((tm, D_MODEL), BF16),
        ],
        args=(x, w_in, w_conv, w_out),
    )


def _conv_sample_kernel(x_ref, prev_ref, win_ref, wconv_ref, wout_ref, g_ref, b_ref,
                        y_ref, state_ref):
    x = x_ref[:, 0, :]
    xb = x.astype(BF16)
    b_gate = _dot(xb, win_ref[:, 0 * D_MODEL:1 * D_MODEL])
    c_gate = _dot(xb, win_ref[:, 1 * D_MODEL:2 * D_MODEL])
    h = _dot(xb, win_ref[:, 2 * D_MODEL:3 * D_MODEL])
    z = _dot(xb, win_ref[:, 3 * D_MODEL:4 * D_MODEL])
    u = c_gate * h
    prev0 = prev_ref[:, 0, :]
    prev1 = prev_ref[:, 1, :]
    conv = wconv_ref[0:1, :] * prev0 + wconv_ref[1:2, :] * prev1 + wconv_ref[2:3, :] * u
    gate = (b_gate * conv * _silu(z)).astype(BF16)
    y = _dot(gate, wout_ref[...])
    y_ref[...] = _layernorm(DN_ALPHA * x + y, g_ref[...], b_ref[...])
    state_ref[:, 0, :] = prev1
    state_ref[:, 1, :] = u


def _rope(x, cos_ref, sin_up_ref, sin_dn_ref):
    cos, sin_up, sin_dn = cos_ref[...], sin_up_ref[...], sin_dn_ref[...]
    heads = []
    for hd in range(N_HEADS):
        xh = x[:, hd * HEAD_DIM:(hd + 1) * HEAD_DIM]
        heads.append(xh * cos
                     + pltpu.roll(xh, ROT_HALF, 1) * sin_up
                     + pltpu.roll(xh, HEAD_DIM - ROT_HALF, 1) * sin_dn)
    return jnp.concatenate(heads, axis=1)


def _store_paged(ref, val):
    groups = PAGE_SIZE // SUBLANES
    sub = lax.broadcasted_iota(jnp.int32, (groups, SUBLANES, HEAD_DIM), 1)
    for pg in range(ref.shape[0]):
        slab = val[pg * PAGE_SIZE:(pg + 1) * PAGE_SIZE]
        parts = [slab[:, hd * HEAD_DIM:(hd + 1) * HEAD_DIM].reshape(groups, SUBLANES, HEAD_DIM)
                 for hd in range(N_HEADS)]
        dist = N_HEADS // 2
        while dist >= 1:
            keep = (sub & dist) == 0
            nxt = list(parts)
            for a in range(N_HEADS):
                if a & dist == 0:
                    b = a + dist
                    nxt[a] = jnp.where(keep, parts[a], pltpu.roll(parts[b], dist, 1))
                    nxt[b] = jnp.where(keep, pltpu.roll(parts[a], SUBLANES - dist, 1), parts[b])
            parts = nxt
            dist //= 2
        page = jnp.stack(parts, axis=1).reshape(PAGE_SIZE, N_HEADS, HEAD_DIM)
        ref[pg] = page.astype(ref.dtype)


def _attn_proj_prompt_kernel(r_ref, g_ref, b_ref, win_ref, cos_ref, sin_up_ref, sin_dn_ref,
                             q_ref, k_ref, v_ref, kb_ref, vb_ref, sz_ref, kmean_ref, *, tm):
    xb = _layernorm(r_ref[0], g_ref[...], b_ref[...]).astype(BF16)
    q = _rope(_dot(xb, win_ref[:, 0 * D_MODEL:1 * D_MODEL]), cos_ref, sin_up_ref, sin_dn_ref)
    q_ref[0] = (q * (ATTN_SCALE * LOG2E)).astype(BF16)
    yield
    k = _rope(_dot(xb, win_ref[:, 1 * D_MODEL:2 * D_MODEL]), cos_ref, sin_up_ref, sin_dn_ref)
    _store_paged(k_ref.at[0], k)
    kb_ref[0] = k.astype(BF16)
    t = pl.program_id(1)
    for j in range(tm // MOBA_BLOCK):
        blk = k[j * MOBA_BLOCK:(j + 1) * MOBA_BLOCK]
        kmean_ref[0, pl.ds(t * (tm // MOBA_BLOCK) + j, 1), :] = (
            jnp.sum(blk, axis=0, keepdims=True) * (1.0 / MOBA_BLOCK))
    yield
    v = _dot(xb, win_ref[:, 2 * D_MODEL:3 * D_MODEL])
    _store_paged(v_ref.at[0], v)
    vb_ref[0] = v.astype(BF16)
    yield
    z = _dot(xb, win_ref[:, 3 * D_MODEL:4 * D_MODEL])
    sz_ref[0] = _silu(z).astype(BF16)


def _attn_proj_prompt(r, g, b, w_in, cos, sin_up, sin_dn, stream):
    bsz, seq, _ = r.shape
    tm = ROW_TILE
    assert seq % tm == 0 and tm % MOBA_BLOCK == 0 and seq // MOBA_BLOCK == SUBLANES
    row_spec = pl.BlockSpec((1, tm, D_MODEL), lambda i, j, *_: (i, j, 0))
    tab_spec = pl.BlockSpec((tm, HEAD_DIM), lambda i, j, *_: (j, 0))
    paged_spec = pl.BlockSpec((1, tm // PAGE_SIZE, PAGE_SIZE, N_HEADS, HEAD_DIM),
                              lambda i, j, *_: (i, j, 0, 0, 0))
    flat = lambda dt: jax.ShapeDtypeStruct((bsz, seq, D_MODEL), dt)
    paged = jax.ShapeDtypeStruct((bsz, seq // PAGE_SIZE, PAGE_SIZE, N_HEADS, HEAD_DIM), F32)
    return _hosted_call(
        functools.partial(_attn_proj_prompt_kernel, tm=tm), stream,
        name="attn_proj_prompt",
        grid=(bsz, seq // tm),
        in_specs=[row_spec, _const_spec((1, D_MODEL)), _const_spec((1, D_MODEL)),
                  _const_spec((D_MODEL, 4 * D_MODEL)), tab_spec, tab_spec, tab_spec],
        out_specs=[row_spec, paged_spec, paged_spec, row_spec, row_spec, row_spec,
                   pl.BlockSpec((1, SUBLANES, D_MODEL), lambda i, j, *_: (i, 0, 0))],
        out_shape=[flat(BF16), paged, paged, flat(BF16), flat(BF16), flat(BF16),
                   jax.ShapeDtypeStruct((bsz, SUBLANES, D_MODEL), F32)],
        scratch_shapes=[],
        args=(r, g, b, w_in, cos, sin_up, sin_dn),
    )


def _attn_proj_sample_kernel(x_ref, win_ref, cos_ref, sin_up_ref, sin_dn_ref,
                             q_ref, k_ref, v_ref, sz_ref):
    xb = x_ref[...].astype(BF16)
    q = _rope(_dot(xb, win_ref[:, 0 * D_MODEL:1 * D_MODEL]), cos_ref, sin_up_ref, sin_dn_ref)
    _store_paged(q_ref, q * (ATTN_SCALE * LOG2E))
    k = _rope(_dot(xb, win_ref[:, 1 * D_MODEL:2 * D_MODEL]), cos_ref, sin_up_ref, sin_dn_ref)
    _store_paged(k_ref, k)
    _store_paged(v_ref, _dot(xb, win_ref[:, 2 * D_MODEL:3 * D_MODEL]))
    _store_paged(sz_ref, _silu(_dot(xb, win_ref[:, 3 * D_MODEL:4 * D_MODEL])))


def _sample_front_kernel(x_ref, prev_ref, winc_ref, wconv_ref, woutc_ref, g_ref, b_ref,
                         wina_ref, cos_ref, sin_up_ref, sin_dn_ref,
                         y_ref, state_ref, q_ref, k_ref, v_ref, sz_ref):
    _conv_sample_kernel(x_ref, prev_ref, winc_ref, wconv_ref, woutc_ref, g_ref, b_ref,
                        y_ref, state_ref)
    _attn_proj_sample_kernel(y_ref, wina_ref, cos_ref, sin_up_ref, sin_dn_ref,
                             q_ref, k_ref, v_ref, sz_ref)


def _sample_front(x, prev, w_in_c, w_conv, w_out_c, g, b, w_in_a, cos, sin_up, sin_dn):
    rows = x.shape[0]
    assert rows % PAGE_SIZE == 0
    paged = jax.ShapeDtypeStruct((rows // PAGE_SIZE, PAGE_SIZE, N_HEADS, HEAD_DIM), F32)
    return pl.pallas_call(
        _sample_front_kernel,
        out_shape=[jax.ShapeDtypeStruct((rows, D_MODEL), F32),
                   jax.ShapeDtypeStruct((rows, CONV_W - 1, D_MODEL), F32),
                   paged, paged, paged, paged],
        compiler_params=pltpu.CompilerParams(vmem_limit_bytes=VMEM_LIMIT),
        name="sample_front",
    )(x, prev, w_in_c, w_conv, w_out_c, g, b, w_in_a, cos, sin_up, sin_dn)


def _moba_prompt_kernel(q_ref, k_ref, v_ref, kmean_ref, sz_ref, o_ref, *, n_blocks):
    blk = MOBA_BLOCK
    kmean = kmean_ref[0].astype(BF16)
    v_t = jnp.concatenate([v_ref[0].T,
                           jnp.ones((BF16_SUBLANES, v_ref.shape[1]), BF16)], axis=0)
    blk_id = lax.broadcasted_iota(jnp.int32, (n_blocks, blk), 0)
    key = lax.broadcasted_iota(jnp.int32, (blk, blk), 0)
    qry = lax.broadcasted_iota(jnp.int32, (blk, blk), 1)
    causal = key <= qry
    for i in range(n_blocks):
        qi = q_ref[0, i * blk:(i + 1) * blk, :]
        scores = []
        if i > MOBA_TOP_K:
            gate = _dot_nt(kmean, qi)
        for n in range(i):
            s = _dot_nt(k_ref[0, n * blk:(n + 1) * blk, :], qi)
            if i > MOBA_TOP_K:
                gn = gate[n:n + 1, :]
                beats = ((gate > gn) | ((gate == gn) & (blk_id < n))) & (blk_id < i)
                rank = jnp.sum(beats.astype(F32), axis=0, keepdims=True)
                s = jnp.where(rank < MOBA_TOP_K, s, -jnp.inf)
            scores.append(s)
        s_own = _dot_nt(k_ref[0, i * blk:(i + 1) * blk, :], qi)
        scores.append(jnp.where(causal, s_own, -jnp.inf))
        m = scores[0].max(axis=0, keepdims=True)
        for s in scores[1:]:
            m = jnp.maximum(m, s.max(axis=0, keepdims=True))
        acc = jnp.zeros((HEAD_DIM + BF16_SUBLANES, blk), F32)
        for n, s in enumerate(scores):
            p = jnp.exp2(s - m).astype(BF16)
            acc = acc + _dot(v_t[:, n * blk:(n + 1) * blk], p)
        o = (acc[:HEAD_DIM] / acc[HEAD_DIM:HEAD_DIM + 1]).T
        o_ref[0, i * blk:(i + 1) * blk, :] = (
            o * sz_ref[0, i * blk:(i + 1) * blk, :].astype(F32)).astype(BF16)
        yield


def _moba_prompt(q, k, v, kmean, sz, stream):
    bsz, seq, _ = q.shape
    n_blocks = seq // MOBA_BLOCK
    assert n_blocks == SUBLANES
    head_spec = pl.BlockSpec((1, seq, HEAD_DIM), lambda i, j, *_: (i, 0, j))
    return _hosted_call(
        functools.partial(_moba_prompt_kernel, n_blocks=n_blocks), stream,
        name="moba_prompt",
        grid=(bsz, N_HEADS),
        in_specs=[head_spec, head_spec, head_spec,
                  pl.BlockSpec((1, n_blocks, HEAD_DIM), lambda i, j, *_: (i, 0, j)),
                  head_spec],
        out_specs=[head_spec],
        out_shape=[jax.ShapeDtypeStruct((bsz, seq, D_MODEL), BF16)],
        scratch_shapes=[],
        args=(q, k, v, kmean, sz),
    )


def _sum_tokens(x):
    t = x.shape[0]
    parts = jnp.sum(x.reshape((STREAM_SUM_PARTS, t // STREAM_SUM_PARTS) + x.shape[1:]), axis=1)
    return jnp.sum(parts, axis=0)


def _stream_body(step, q_ref, kn_ref, vn_ref, sz_ref, k_refs, v_refs,
                 o_ref, m_ref, l_ref, acc_ref, gate_ref, *, n_blocks, blocks_per_step):
    tile = (N_HEADS, HEAD_DIM)
    q = q_ref[0]
    for j in range(blocks_per_step):
        m = jnp.full((N_HEADS, 1), -jnp.inf, F32)
        l = jnp.zeros((N_HEADS, 1), F32)
        acc = jnp.zeros(tile, F32)
        ksum = jnp.zeros(tile, F32)
        for pg in range(j * PAGES_PER_BLOCK, (j + 1) * PAGES_PER_BLOCK):
            for c in range(PAGE_SIZE // STREAM_CHUNK):
                tok = slice(c * STREAM_CHUNK, (c + 1) * STREAM_CHUNK)
                kc = k_refs[pg][tok]
                s = jnp.sum(kc * q[None], axis=-1, keepdims=True)
                m_new = jnp.maximum(m, jnp.max(s, axis=0))
                alpha = jnp.exp2(m - m_new)
                p = jnp.exp2(s - m_new[None])
                l = l * alpha + _sum_tokens(p)
                acc = acc * alpha + _sum_tokens(p * v_refs[pg][tok])
                ksum = ksum + _sum_tokens(kc)
                m = m_new
        kmean = ksum * (1.0 / MOBA_BLOCK)
        n = step * blocks_per_step + j
        m_ref[n] = jnp.broadcast_to(m, tile)
        l_ref[n] = jnp.broadcast_to(l, tile)
        acc_ref[n] = acc
        gate_ref[n] = jnp.broadcast_to(jnp.sum(q * kmean, axis=-1, keepdims=True), tile)
        yield

    @pl.when(step == n_blocks // blocks_per_step - 1)
    def _():
        gates = gate_ref[...]
        blk_id = lax.broadcasted_iota(jnp.int32, (n_blocks,) + tile, 0)
        sel = jnp.zeros((n_blocks,) + tile, jnp.bool_)
        for j in range(n_blocks):
            gj = gates[j:j + 1]
            beats = (gates > gj) | ((gates == gj) & (blk_id < j))
            rank = jnp.sum(beats.astype(F32), axis=0, keepdims=True)
            sel = sel | ((blk_id == j) & (rank < MOBA_TOP_K))
        s_own = jnp.broadcast_to(jnp.sum(q * kn_ref[0], axis=-1, keepdims=True), tile)
        ms = jnp.where(sel, m_ref[...], -jnp.inf)
        m_fin = jnp.maximum(jnp.max(ms, axis=0), s_own)
        w = jnp.exp2(ms - m_fin[None])
        w_own = jnp.exp2(s_own - m_fin)
        denom = jnp.sum(w * l_ref[...], axis=0) + w_own
        numer = jnp.sum(w * acc_ref[...], axis=0) + w_own * vn_ref[0]
        o_ref[0] = (numer / denom) * sz_ref[0]


class _Stream(NamedTuple):
    page_table: jax.Array
    q: jax.Array
    k_new: jax.Array
    v_new: jax.Array
    sz: jax.Array
    cache_k: jax.Array
    cache_v: jax.Array
    first_seq: int
    n_seqs: int
    blocks_per_step: int


def _hosted_call(host_kernel, stream, *, name, grid, in_specs, out_specs, out_shape,
                 scratch_shapes, args):
    n_blocks = PAST_LEN // MOBA_BLOCK
    bps = stream.blocks_per_step
    groups = n_blocks // bps
    n_pages = bps * PAGES_PER_BLOCK
    g0, g1 = grid
    assert PAST_LEN % MOBA_BLOCK == 0 and n_blocks % bps == 0 and g0 * g1 == stream.n_seqs * groups
    assert stream.page_table.shape[1] == n_blocks * PAGES_PER_BLOCK
    assert stream.cache_k.shape[0] == 1 and stream.cache_k.shape[2:] == (PAGE_SIZE, N_HEADS, HEAD_DIM)

    n_steps = g0 * g1
    first_seq = stream.first_seq

    def seq_of(u):
        return u // groups

    def step_of(i, j):
        return i * g1 + j

    row_spec = pl.BlockSpec((1, N_HEADS, HEAD_DIM),
                            lambda i, j, pt: (first_seq + seq_of(step_of(i, j)), 0, 0))
    hbm_spec = pl.BlockSpec(memory_space=pl.ANY)
    page_buf = pltpu.VMEM((2, n_pages, PAGE_SIZE, N_HEADS, HEAD_DIM), F32)
    stat = pltpu.VMEM((n_blocks, N_HEADS, HEAD_DIM), F32)
    n_in, n_out, n_scr = len(in_specs), len(out_specs), len(scratch_shapes)

    def body(pt_ref, *refs):
        host_in, refs = refs[:n_in], refs[n_in:]
        rows, (ck_ref, cv_ref), refs = refs[:4], refs[4:6], refs[6:]
        host_out, o_ref, refs = refs[:n_out], refs[n_out], refs[n_out + 1:]
        host_scr, (kbuf, vbuf, sem), stats = refs[:n_scr], refs[n_scr:n_scr + 3], refs[n_scr + 3:]
        u = step_of(pl.program_id(0), pl.program_id(1))
        slot = u % 2

        def page_copies(step, to_slot):
            seq, first_page = first_seq + seq_of(step), (step % groups) * n_pages
            copies = []
            for pg in range(n_pages):
                page = pt_ref[seq, first_page + pg]
                copies.append(pltpu.make_async_copy(
                    ck_ref.at[0, page], kbuf.at[to_slot, pg], sem.at[to_slot, 0]))
                copies.append(pltpu.make_async_copy(
                    cv_ref.at[0, page], vbuf.at[to_slot, pg], sem.at[to_slot, 1]))
            return copies

        def start_all(copies):
            for c in copies:
                c.start(priority=BULK_DMA_THREAD)

        @pl.when(u == 0)
        def _():
            start_all(page_copies(u, slot))

        @pl.when(u + 1 < n_steps)
        def _():
            start_all(page_copies(u + 1, 1 - slot))

        for c in page_copies(u, slot):
            c.wait()

        host = host_kernel(*host_in, *host_out, *host_scr)
        k_refs = [kbuf.at[slot, pg] for pg in range(n_pages)]
        v_refs = [vbuf.at[slot, pg] for pg in range(n_pages)]
        strm = _stream_body(u % groups, *rows, k_refs, v_refs, o_ref, *stats,
                            n_blocks=n_blocks, blocks_per_step=bps)
        for _ in itertools.zip_longest(host, strm):
            pass

    res = pl.pallas_call(
        body,
        grid_spec=pltpu.PrefetchScalarGridSpec(
            num_scalar_prefetch=1,
            grid=grid,
            in_specs=list(in_specs) + [row_spec] * 4 + [hbm_spec, hbm_spec],
            out_specs=list(out_specs) + [
                pl.BlockSpec((1, N_HEADS, HEAD_DIM),
                             lambda i, j, pt: (seq_of(step_of(i, j)), 0, 0))],
            scratch_shapes=list(scratch_shapes) + [
                page_buf, page_buf, pltpu.SemaphoreType.DMA((2, 2))] + [stat] * 4,
        ),
        out_shape=list(out_shape) + [jax.ShapeDtypeStruct((stream.n_seqs, N_HEADS, HEAD_DIM), F32)],
        compiler_params=pltpu.CompilerParams(
            dimension_semantics=("arbitrary", "arbitrary"),
            vmem_limit_bytes=VMEM_LIMIT),
        name=name,
    )(stream.page_table, *args, stream.q, stream.k_new, stream.v_new, stream.sz,
      stream.cache_k, stream.cache_v)
    return res[:-1], res[-1]


def _out_proj_kernel(a_ref, x_ref, w_ref, g_ref, b_ref, *rest):
    y_ref = rest[-1]
    x = x_ref[...]
    if len(rest) == 3:
        x = _layernorm(x, rest[0][...], rest[1][...])
    y = _dot(a_ref[...].astype(BF16), w_ref[...])
    y_ref[...] = _layernorm(DN_ALPHA * x + y, g_ref[...], b_ref[...])


def _out_proj(a, x, w, g, b, *, tm, norm_in=None):
    rows = x.shape[0]
    assert rows % tm == 0
    row_spec = pl.BlockSpec((tm, D_MODEL), lambda i: (i, 0))
    vec_spec = _const_spec((1, D_MODEL))
    extra = () if norm_in is None else tuple(norm_in)
    return pl.pallas_call(
        _out_proj_kernel,
        grid=(rows // tm,),
        in_specs=[row_spec, row_spec, _const_spec((D_MODEL, D_MODEL)), vec_spec, vec_spec]
        + [vec_spec] * len(extra),
        out_specs=row_spec,
        out_shape=jax.ShapeDtypeStruct((rows, D_MODEL), F32),
        compiler_params=pltpu.CompilerParams(
            dimension_semantics=("arbitrary",), vmem_limit_bytes=VMEM_LIMIT),
        name="out_proj",
    )(a, x, w, g, b, *extra)


def _rope_tables(pos):
    inv = ROPE_THETA ** (-np.arange(ROT_HALF, dtype=np.float64) * 2.0 / ROT_DIM)
    ang = np.asarray(pos, np.float64)[:, None] * inv[None, :]
    cos, sin = np.cos(ang), np.sin(ang)
    n = ang.shape[0]
    rest = HEAD_DIM - ROT_DIM
    cos_t = np.concatenate([cos, cos, np.ones((n, rest))], axis=1)
    sin_up = np.concatenate([np.zeros((n, ROT_HALF)), sin, np.zeros((n, rest))], axis=1)
    sin_dn = np.concatenate([-sin, np.zeros((n, ROT_HALF + rest))], axis=1)
    return tuple(jnp.asarray(t, F32) for t in (cos_t, sin_up, sin_dn))


def kernel(x_prompt, x_sample, state_conv, cache_k, cache_v, page_table,
           w_in_conv, w_conv, w_out_conv, w_in_attn, w_out_attn, ln_g, ln_b):
    bsz, seq, _ = x_prompt.shape
    rows, dec_seq, _ = x_sample.shape
    assert dec_seq == 1 and w_in_conv.shape[0] == 1 and w_in_attn.shape[0] == 1
    win_c, wout_c = w_in_conv[0].astype(BF16), w_out_conv[0].astype(BF16)
    win_a, wout_a = w_in_attn[0].astype(BF16), w_out_attn[0].astype(BF16)
    g0, b0, g1, b1 = ln_g[0:1], ln_b[0:1], ln_g[1:2], ln_b[1:2]

    assert CONV_W == 3
    xs1, conv_s, qs, ks, vs, szs = _sample_front(
        x_sample, state_conv[0], win_c, w_conv[0], wout_c, g0, b0, win_a,
        *_rope_tables(np.full((rows,), PAST_LEN)))
    conv_s = conv_s[None]
    per_row = lambda t: t.reshape(rows, N_HEADS, HEAD_DIM)
    n_blocks = PAST_LEN // MOBA_BLOCK
    quarter = rows // 4

    def stream(first_seq, n_seqs, blocks_per_step):
        return _Stream(page_table, per_row(qs), per_row(ks), per_row(vs), per_row(szs),
                       cache_k, cache_v, first_seq, n_seqs, blocks_per_step)

    (rp0, conv_tail), o_a = _conv_prompt(x_prompt, win_c, w_conv[0], wout_c,
                                         stream(0, quarter, n_blocks // 2))
    conv_p = conv_tail[:, :, SUBLANES - (CONV_W - 1):, :]
    (q, k_p, v_p, kb, vb, sz, kmean), o_b = _attn_proj_prompt(
        rp0, g0, b0, win_a, *_rope_tables(np.arange(seq)), stream(quarter, quarter, n_blocks // 2))
    (o,), o_c = _moba_prompt(q, kb, vb, kmean, sz, stream(2 * quarter, 2 * quarter, n_blocks))
    y_p = _out_proj(o.reshape(bsz * seq, D_MODEL), rp0.reshape(bsz * seq, D_MODEL),
                    wout_a, g1, b1, tm=OUT_ROW_TILE, norm_in=(g0, b0))
    y_p = y_p.reshape(bsz, seq, D_MODEL)

    os_ = jnp.concatenate([o_a, o_b, o_c], axis=0)
    y_s = _out_proj(os_.reshape(rows, D_MODEL), xs1, wout_a, g1, b1, tm=rows)
    y_s = y_s.reshape(rows, 1, D_MODEL)
    new_shape = (1, rows, 1, N_HEADS, HEAD_DIM)
    return (y_p, y_s, conv_p, conv_s, k_p[None], v_p[None],
            ks.reshape(new_shape), vs.reshape(new_shape))
```
